```python
import math
import jax, jax.numpy as jnp
from jax import lax
import numpy as np

D_MODEL = 1024
BATCH = 8
SEQ = 2048
DEPTH = 4
DEC_BATCH = 128
DEC_SEQ = 1
PAST_LEN = 16384
PAGE_SIZE = 128

MIX_WIDTH = D_MODEL
CONV_CH = MIX_WIDTH // 2
CONV_K = 31
RET_HEADS = 4
RET_DK = (MIX_WIDTH - CONV_CH) // RET_HEADS
RET_DV = (MIX_WIDTH - CONV_CH) // RET_HEADS
RET_QK = RET_HEADS * RET_DK
RET_V = RET_HEADS * RET_DV
RET_CHUNK = 128
ROPE_BASE = 10000.0
D_FF = ((8 * D_MODEL // 3 + 255) // 256) * 256
IN_COLS = 2 * CONV_CH + 2 * RET_QK + 2 * RET_V
DN_ALPHA = (2 * DEPTH) ** 0.25
DN_BETA = (8 * DEPTH) ** -0.25
LN_EPS = 1e-5

kernel_name = "hymba_conformer_retnet_deepnorm_step"


def layer_norm(x, g, b):
    xf = x.astype(jnp.float32)
    mu = jnp.mean(xf, axis=-1, keepdims=True)
    var = jnp.mean(jnp.square(xf - mu), axis=-1, keepdims=True)
    y = (xf - mu) * lax.rsqrt(var + LN_EPS) * g.astype(jnp.float32) + b.astype(jnp.float32)
    return y.astype(x.dtype)


def rotary(x, pos):
    d = x.shape[-1]
    inv_freq = ROPE_BASE ** (-jnp.arange(0, d // 2, dtype=jnp.float32) / (d // 2))
    ang = pos.astype(jnp.float32)[:, None] * inv_freq[None, :]
    cos = jnp.cos(ang)[None, :, None, :]
    sin = jnp.sin(ang)[None, :, None, :]
    xf = x.astype(jnp.float32)
    x1, x2 = xf[..., : d // 2], xf[..., d // 2:]
    return jnp.concatenate([x1 * cos - x2 * sin, x1 * sin + x2 * cos], axis=-1)


def retention_chunkwise(q, k, v, s0, log_g, chunk):
    B, L, H, dk = q.shape
    dv = v.shape[-1]
    n = L // chunk
    idx = jnp.arange(chunk, dtype=jnp.float32)
    diff = idx[:, None] - idx[None, :]
    dmask = jnp.where(diff[None] >= 0,
                      jnp.exp(jnp.maximum(diff, 0.0)[None] * log_g[:, None, None]), 0.0)
    q_decay = jnp.exp((idx + 1.0)[:, None] * log_g[None, :])
    k_decay = jnp.exp((chunk - 1.0 - idx)[:, None] * log_g[None, :])
    chunk_decay = jnp.exp(chunk * log_g)

    def to_chunks(t):
        return jnp.moveaxis(t.reshape(B, n, chunk, H, t.shape[-1]), 1, 0)

    def step(s, xs):
        qb, kb, vb = xs
        sc = jnp.einsum('bihd,bjhd->bhij', qb, kb) * dmask[None]
        o = (jnp.einsum('bhij,bjhv->bihv', sc, vb)
             + jnp.einsum('bihd,bhdv->bihv', qb, s) * q_decay[None, :, :, None])
        s_new = (s * chunk_decay[None, :, None, None]
                 + jnp.einsum('bjhd,bjhv->bhdv', kb * k_decay[None, :, :, None], vb))
        return s_new, o

    s_fin, o = lax.scan(step, s0, (to_chunks(q), to_chunks(k), to_chunks(v)))
    o = jnp.moveaxis(o, 0, 1).reshape(B, L, H, dv)
    return o, s_fin


def depthwise_causal_conv(full, w, b):
    out = lax.conv_general_dilated(full, w[:, None, :].astype(full.dtype),
                                   window_strides=(1,), padding='VALID',
                                   dimension_numbers=('NWC', 'WIO', 'NWC'),
                                   feature_group_count=full.shape[-1])
    return out + b.astype(full.dtype)


def mixer(h, pos, conv_buf, s0, chunk, w_in, conv_w, conv_b, conv_ln_g, conv_ln_b, ret_gn_g, w_out):
    B, L, _ = h.shape
    z = h @ w_in
    o1 = CONV_CH
    o2 = o1 + CONV_CH
    o3 = o2 + RET_QK
    o4 = o3 + RET_QK
    o5 = o4 + RET_V
    ca, cb = z[..., :o1], z[..., o1:o2]
    q = z[..., o2:o3].reshape(B, L, RET_HEADS, RET_DK)
    k = z[..., o3:o4].reshape(B, L, RET_HEADS, RET_DK)
    v = z[..., o4:o5].reshape(B, L, RET_HEADS, RET_DV)
    gate = z[..., o5:]

    u = ca * jax.nn.sigmoid(cb)
    full = jnp.concatenate([conv_buf.astype(u.dtype), u], axis=1)
    c = depthwise_causal_conv(full, conv_w, conv_b)
    c = jax.nn.silu(layer_norm(c, conv_ln_g, conv_ln_b))
    new_buf = full[:, -(CONV_K - 1):]

    log_g = jnp.log(1.0 - 2.0 ** (-5.0 - jnp.arange(RET_HEADS, dtype=jnp.float32)))
    qr = rotary(q, pos)
    kr = rotary(k, pos) * (RET_DK ** -0.5)
    o, s_new = retention_chunkwise(qr, kr, v.astype(jnp.float32), s0.astype(jnp.float32), log_g, chunk)
    mu = jnp.mean(o, axis=-1, keepdims=True)
    var = jnp.mean(jnp.square(o - mu), axis=-1, keepdims=True)
    o = ((o - mu) * lax.rsqrt(var + LN_EPS)).reshape(B, L, RET_V) * ret_gn_g.astype(jnp.float32)
    r = (jax.nn.silu(gate.astype(jnp.float32)) * o).astype(h.dtype)

    out = jnp.concatenate([c.astype(h.dtype), r], axis=-1) @ w_out
    return out, new_buf, s_new.astype(s0.dtype)


def trunk(x, pos, conv_bufs, ret_states, chunk, w_in, conv_w, conv_b, conv_ln_g, conv_ln_b,
          ret_gn_g, w_out, ln1_g, ln1_b, w_gate_up, w_down, ln2_g, ln2_b):
    new_bufs, new_states = [], []
    for l in range(DEPTH):
        m, nb, ns = mixer(x, pos, conv_bufs[l], ret_states[l], chunk, w_in[l], conv_w[l], conv_b[l],
                          conv_ln_g[l], conv_ln_b[l], ret_gn_g[l], w_out[l])
        new_bufs.append(nb)
        new_states.append(ns)
        x = layer_norm(DN_ALPHA * x + m, ln1_g[l], ln1_b[l])
        gu = x @ w_gate_up[l]
        f = (jax.nn.silu(gu[..., :D_FF]) * gu[..., D_FF:]) @ w_down[l]
        x = layer_norm(DN_ALPHA * x + f, ln2_g[l], ln2_b[l])
    return x, jnp.stack(new_bufs, axis=0), jnp.stack(new_states, axis=0)


def setup_inputs(seed: int = 0) -> dict:
    key = jax.random.key(seed)
    ks = jax.random.split(key, 20)
    f32 = jnp.float32
    nrm = lambda k, s, sc: jax.random.normal(k, s, f32) * sc
    return {
        "x_prompt": nrm(ks[0], (BATCH, SEQ, D_MODEL), 1.0),
        "x_sample": nrm(ks[1], (DEC_BATCH, DEC_SEQ, D_MODEL), 1.0),
        "state_conv": nrm(ks[2], (DEPTH, DEC_BATCH, CONV_K - 1, CONV_CH), 0.5),
        "state_ret": nrm(ks[3], (DEPTH, DEC_BATCH, RET_HEADS, RET_DK, RET_DV), 0.3),
        "w_in": nrm(ks[4], (DEPTH, D_MODEL, IN_COLS), D_MODEL ** -0.5),
        "conv_w": nrm(ks[5], (DEPTH, CONV_K, CONV_CH), CONV_K ** -0.5),
        "conv_b": nrm(ks[6], (DEPTH, CONV_CH), 0.02),
        "conv_ln_g": 1.0 + nrm(ks[7], (DEPTH, CONV_CH), 0.02),
        "conv_ln_b": nrm(ks[8], (DEPTH, CONV_CH), 0.02),
        "ret_gn_g": 1.0 + nrm(ks[9], (DEPTH, RET_V), 0.02),
        "w_out": nrm(ks[10], (DEPTH, MIX_WIDTH, D_MODEL), DN_BETA * MIX_WIDTH ** -0.5),
        "ln1_g": 1.0 + nrm(ks[11], (DEPTH, D_MODEL), 0.02),
        "ln1_b": nrm(ks[12], (DEPTH, D_MODEL), 0.02),
        "w_gate_up": nrm(ks[13], (DEPTH, D_MODEL, 2 * D_FF), D_MODEL ** -0.5),
        "w_down": nrm(ks[14], (DEPTH, D_FF, D_MODEL), DN_BETA * D_FF ** -0.5),
        "ln2_g": 1.0 + nrm(ks[15], (DEPTH, D_MODEL), 0.02),
        "ln2_b": nrm(ks[16], (DEPTH, D_MODEL), 0.02),
    }


def reference(x_prompt, x_sample, state_conv, state_ret, w_in, conv_w, conv_b, conv_ln_g, conv_ln_b,
              ret_gn_g, w_out, ln1_g, ln1_b, w_gate_up, w_down, ln2_g, ln2_b):
    p_bufs = jnp.zeros((DEPTH, BATCH, CONV_K - 1, CONV_CH), x_prompt.dtype)
    p_states = jnp.zeros((DEPTH, BATCH, RET_HEADS, RET_DK, RET_DV), state_ret.dtype)
    p_pos = jnp.arange(SEQ)
    y_prompt, conv_prompt, ret_prompt = trunk(
        x_prompt, p_pos, p_bufs, p_states, min(RET_CHUNK, SEQ), w_in, conv_w, conv_b, conv_ln_g,
        conv_ln_b, ret_gn_g, w_out, ln1_g, ln1_b, w_gate_up, w_down, ln2_g, ln2_b)
    s_pos = PAST_LEN + jnp.arange(DEC_SEQ)
    y_sample, conv_sample, ret_sample = trunk(
        x_sample, s_pos, state_conv, state_ret, DEC_SEQ, w_in, conv_w, conv_b, conv_ln_g,
        conv_ln_b, ret_gn_g, w_out, ln1_g, ln1_b, w_gate_up, w_down, ln2_g, ln2_b)
    return (y_prompt, y_sample, conv_prompt, ret_prompt, conv_sample, ret_sample)
```

```python
import functools

import jax
import jax.numpy as jnp
from jax import lax
from jax.experimental import pallas as pl
from jax.experimental.pallas import tpu as pltpu

F32 = jnp.float32
BF16 = jnp.bfloat16

D_MODEL = 1024
DEPTH = 4
CONV_CH = 512
CONV_K = 31
HEADS = 4
HEAD_DIM = 128
RET_W = HEADS * HEAD_DIM
D_FF = 2816
IN_COLS = 2 * CONV_CH + 4 * RET_W
PAST_LEN = 16384
ROPE_BASE = 10000.0
DN_ALPHA = (2 * DEPTH) ** 0.25
LN_EPS = 1e-5
CHUNK = 128

VMEM_LIMIT_BYTES = 56 * 1024 * 1024

HIST = 32
CONV_ROWS = 32
FF_CHUNKS = tuple((s, min(512, D_FF - s)) for s in range(0, D_FF, 512))


def _dot(a, b):
    return jnp.dot(a, b, preferred_element_type=F32)


def _dot_nt(a, b):
    return lax.dot_general(a, b, (((1,), (1,)), ((), ())), preferred_element_type=F32)


def _layer_norm(y, g, b):
    mu = jnp.mean(y, axis=-1, keepdims=True)
    d = y - mu
    var = jnp.mean(d * d, axis=-1, keepdims=True)
    return d * lax.rsqrt(var + LN_EPS) * g + b


def _silu(x):
    return x * jax.nn.sigmoid(x)


def _rotary(seg, cos, sin_signed):
    return seg * cos + pltpu.roll(seg, HEAD_DIM // 2, 1) * sin_signed


def _mix_in_kernel(x_ref, w_ref, cq_ref, sq_ref, ck_ref, sk_ref,
                   u_ref, q_ref, k_ref, v_ref, sg_ref):
    xb = x_ref[...].astype(BF16)
    ca = _dot(xb, w_ref[:, 0:CONV_CH])
    cb = _dot(xb, w_ref[:, CONV_CH:2 * CONV_CH])
    u_ref[...] = ca * jax.nn.sigmoid(cb)
    o = 2 * CONV_CH
    zq = _dot(xb, w_ref[:, o:o + RET_W])
    zk = _dot(xb, w_ref[:, o + RET_W:o + 2 * RET_W])
    cq, sq, ck, sk = cq_ref[...], sq_ref[...], ck_ref[...], sk_ref[...]
    for h in range(HEADS):
        sl = slice(h * HEAD_DIM, (h + 1) * HEAD_DIM)
        q_ref[:, sl] = _rotary(zq[:, sl], cq, sq).astype(BF16)
        k_ref[:, sl] = _rotary(zk[:, sl], ck, sk)
    v_ref[...] = _dot(xb, w_ref[:, o + 2 * RET_W:o + 3 * RET_W]).astype(BF16)
    sg_ref[...] = _silu(_dot(xb, w_ref[:, o + 3 * RET_W:o + 4 * RET_W]))


def _mix_in(x, w_in_b, layer, tables, tm):
    m = x.shape[0]
    n_pos = tables[0].shape[0] // tm
    row = lambda i: (i, 0)
    pos = lambda i: (i % n_pos, 0)
    tab_spec = pl.BlockSpec((tm, HEAD_DIM), pos)
    out_spec = pl.BlockSpec((tm, RET_W), row)
    return pl.pallas_call(
        _mix_in_kernel,
        grid=(m // tm,),
        in_specs=[pl.BlockSpec((tm, D_MODEL), row),
                  pl.BlockSpec((None, D_MODEL, IN_COLS), lambda i: (layer, 0, 0)),
                  tab_spec, tab_spec, tab_spec, tab_spec],
        out_specs=[out_spec] * 5,
        out_shape=[jax.ShapeDtypeStruct((m, CONV_CH), F32),
                   jax.ShapeDtypeStruct((m, RET_W), BF16),
                   jax.ShapeDtypeStruct((m, RET_W), F32),
                   jax.ShapeDtypeStruct((m, RET_W), BF16),
                   jax.ShapeDtypeStruct((m, RET_W), F32)],
        compiler_params=pltpu.CompilerParams(dimension_semantics=("arbitrary",),
                                             vmem_limit_bytes=VMEM_LIMIT_BYTES),
        name="mix_in",
    )(x, w_in_b, *tables)


def _mix_core_prompt_kernel(u_ref, q_ref, k_ref, v_ref, sg_ref, x_ref,
                            cw_ref, cb_ref, clg_ref, clb_ref, gn_ref, wout_ref, l1g_ref, l1b_ref,
                            dmask_ref, qdec_ref, kdec_ref, cdec_ref,
                            x1_ref, convst_ref, retst_ref,
                            ubuf, s_scr, cr_scr, *, tl):
    l = pl.program_id(1)

    @pl.when(l == 0)
    def _():
        ubuf[0:HIST, :] = jnp.zeros((HIST, CONV_CH), F32)
        s_scr[...] = jnp.zeros_like(s_scr)

    ubuf[HIST:HIST + tl, :] = u_ref[...]

    off = HIST - (CONV_K - 1)
    clg, clb, cbias = clg_ref[...], clb_ref[...], cb_ref[...]
    for rb in range(tl // CONV_ROWS):
        r0 = rb * CONV_ROWS
        acc = jnp.broadcast_to(cbias, (CONV_ROWS, CONV_CH))
        for j in range(CONV_K):
            acc = acc + ubuf[r0 + off + j:r0 + off + j + CONV_ROWS, :] * cw_ref[j:j + 1, :]
        c = _silu(_layer_norm(acc, clg, clb))
        cr_scr[r0:r0 + CONV_ROWS, 0:CONV_CH] = c.astype(BF16)

    @pl.when(l == pl.num_programs(1) - 1)
    def _():
        convst_ref[...] = ubuf[HIST + tl - (CONV_K - 1):HIST + tl, :]

    ubuf[0:HIST, :] = ubuf[tl:tl + HIST, :]

    for c in range(tl // CHUNK):
        rows = slice(c * CHUNK, (c + 1) * CHUNK)
        for h in range(HEADS):
            cols = slice(h * HEAD_DIM, (h + 1) * HEAD_DIM)
            qc = q_ref[rows, cols]
            kc = k_ref[rows, cols]
            vc = v_ref[rows, cols]
            s = s_scr[h]
            sc = _dot_nt(qc, kc.astype(BF16)) * dmask_ref[h]
            o = _dot(sc.astype(BF16), vc) + _dot(qc, s.astype(BF16)) * qdec_ref[h]
            kd_t = (kc * kdec_ref[h]).T.astype(BF16)
            s_scr[h] = s * cdec_ref[h] + _dot(kd_t, vc)
            mu = jnp.mean(o, axis=-1, keepdims=True)
            d = o - mu
            var = jnp.mean(d * d, axis=-1, keepdims=True)
            on = d * lax.rsqrt(var + LN_EPS) * gn_ref[:, cols]
            cr_scr[rows, CONV_CH + h * HEAD_DIM:CONV_CH + (h + 1) * HEAD_DIM] = (
                sg_ref[rows, cols] * on).astype(BF16)

    @pl.when(l == pl.num_programs(1) - 1)
    def _():
        retst_ref[...] = s_scr[...]

    m = _dot(cr_scr[...], wout_ref[...])
    x1_ref[...] = _layer_norm(DN_ALPHA * x_ref[...] + m, l1g_ref[...], l1b_ref[...])


def _mix_core_prompt(u, q, k, v, sg, x, params, layer, decay, batch, seq, tl):
    conv_w, conv_b, conv_ln_g, conv_ln_b, ret_gn_g, w_out_b, ln1_g, ln1_b = params
    nl = seq // tl
    row = lambda b, l: (b * nl + l, 0)
    lay2 = lambda b, l: (layer, 0, 0)
    const3 = lambda b, l: (0, 0, 0)
    act = lambda w: pl.BlockSpec((tl, w), row)
    vec = lambda w: pl.BlockSpec((None, 1, w), lay2)
    dec_spec = pl.BlockSpec((HEADS, CHUNK, CHUNK), const3)
    return pl.pallas_call(
        functools.partial(_mix_core_prompt_kernel, tl=tl),
        grid=(batch, nl),
        in_specs=[act(CONV_CH), act(RET_W), act(RET_W), act(RET_W), act(RET_W), act(D_MODEL),
                  pl.BlockSpec((None, CONV_K, CONV_CH), lay2),
                  vec(CONV_CH), vec(CONV_CH), vec(CONV_CH), vec(RET_W),
                  pl.BlockSpec((None, D_MODEL, D_MODEL), lay2),
                  vec(D_MODEL), vec(D_MODEL),
                  dec_spec, dec_spec, dec_spec, dec_spec],
        out_specs=[act(D_MODEL),
                   pl.BlockSpec((None, CONV_K - 1, CONV_CH), lambda b, l: (b, 0, 0)),
                   pl.BlockSpec((None, HEADS, HEAD_DIM, HEAD_DIM), lambda b, l: (b, 0, 0, 0))],
        out_shape=[jax.ShapeDtypeStruct((batch * seq, D_MODEL), F32),
                   jax.ShapeDtypeStruct((batch, CONV_K - 1, CONV_CH), F32),
                   jax.ShapeDtypeStruct((batch, HEADS, HEAD_DIM, HEAD_DIM), F32)],
        scratch_shapes=[pltpu.VMEM((HIST + tl, CONV_CH), F32),
                        pltpu.VMEM((HEADS, HEAD_DIM, HEAD_DIM), F32),
                        pltpu.VMEM((tl, D_MODEL), BF16)],
        compiler_params=pltpu.CompilerParams(dimension_semantics=("arbitrary", "arbitrary"),
                                             vmem_limit_bytes=VMEM_LIMIT_BYTES),
        name="mix_core_prompt",
    )(u, q, k, v, sg, x, conv_w, conv_b, conv_ln_g, conv_ln_b, ret_gn_g, w_out_b, ln1_g, ln1_b,
      *decay)


def _mix_core_sample_kernel(*refs, tb, aliased):
    if aliased:
        refs = refs[:18] + refs[20:]
    (u_ref, q_ref, kall_ref, vall_ref, sg_ref, x_ref, cst_ref, rst_ref,
     cw_ref, cb_ref, clg_ref, clb_ref, gn_ref, wout_ref, l1g_ref, l1b_ref,
     kdec_ref, cdec_ref,
     x1_ref, cso_ref, rso_ref,
     c_scr) = refs
    i = pl.program_id(0)
    nb = kall_ref.shape[0]

    k_t = [(kall_ref[:, h * HEAD_DIM:(h + 1) * HEAD_DIM] * kdec_ref[h]).T for h in range(HEADS)]
    lane = lax.broadcasted_iota(jnp.int32, (HEAD_DIM, nb), 1)
    sub = lax.broadcasted_iota(jnp.int32, (tb, HEAD_DIM), 0)
    q_tile = q_ref[...].astype(F32)
    w_hist = cw_ref[0:CONV_K - 1, :]

    def body(bb, o_acc):
        b = i * tb + bb
        new_acc = []
        for h in range(HEADS):
            cols = slice(h * HEAD_DIM, (h + 1) * HEAD_DIM)
            lhs = jnp.where(lane == b, k_t[h], 0.0).astype(BF16)
            s_new = rst_ref[bb, h] * cdec_ref[h] + _dot(lhs, vall_ref[:, cols])
            rso_ref[bb, h] = s_new
            ql = jnp.where(sub == bb, q_tile[:, cols], 0.0).astype(BF16)
            new_acc.append(o_acc[h] + _dot(ql, s_new.astype(BF16)))
        hist = cst_ref[bb]
        c_scr[pl.ds(bb, 1), :] = jnp.sum(hist * w_hist, axis=0, keepdims=True)
        cso_ref[bb, 0:CONV_K - 2, :] = cst_ref[bb, 1:CONV_K - 1, :]
        cso_ref[bb, CONV_K - 2:CONV_K - 1, :] = u_ref[pl.ds(bb, 1), :]
        return tuple(new_acc)

    o_acc = lax.fori_loop(0, tb, body, tuple(jnp.zeros((tb, HEAD_DIM), F32) for _ in range(HEADS)))

    conv = c_scr[...] + u_ref[...] * cw_ref[CONV_K - 1:CONV_K, :] + cb_ref[...]
    c = _silu(_layer_norm(conv, clg_ref[...], clb_ref[...])).astype(BF16)
    rs = []
    for h in range(HEADS):
        cols = slice(h * HEAD_DIM, (h + 1) * HEAD_DIM)
        o = o_acc[h]
        mu = jnp.mean(o, axis=-1, keepdims=True)
        d = o - mu
        var = jnp.mean(d * d, axis=-1, keepdims=True)
        on = d * lax.rsqrt(var + LN_EPS) * gn_ref[:, cols]
        rs.append((sg_ref[:, cols] * on).astype(BF16))
    cr = jnp.concatenate([c] + rs, axis=-1)
    m = _dot(cr, wout_ref[...])
    x1_ref[...] = _layer_norm(DN_ALPHA * x_ref[...] + m, l1g_ref[...], l1b_ref[...])


def _mix_core_sample(u, q, k, v, sg, x, state_conv, state_ret, params, layer, decay, bufs, tb):
    conv_w, conv_b, conv_ln_g, conv_ln_b, ret_gn_g, w_out_b, ln1_g, ln1_b = params
    nb = x.shape[0]
    row = lambda i: (i, 0)
    lay2 = lambda i: (layer, 0, 0)
    act = lambda w: pl.BlockSpec((tb, w), row)
    full = lambda w: pl.BlockSpec((nb, w), lambda i: (0, 0))
    vec = lambda w: pl.BlockSpec((None, 1, w), lay2)
    dec_spec = pl.BlockSpec((HEADS, HEAD_DIM, HEAD_DIM), lambda i: (0, 0, 0))
    cst_spec = pl.BlockSpec((None, tb, CONV_K - 1, CONV_CH), lambda i: (layer, i, 0, 0))
    rst_spec = pl.BlockSpec((None, tb, HEADS, HEAD_DIM, HEAD_DIM), lambda i: (layer, i, 0, 0, 0))
    in_specs = [act(CONV_CH), act(RET_W), full(RET_W), full(RET_W), act(RET_W), act(D_MODEL),
                cst_spec, rst_spec,
                pl.BlockSpec((None, CONV_K, CONV_CH), lay2),
                vec(CONV_CH), vec(CONV_CH), vec(CONV_CH), vec(RET_W),
                pl.BlockSpec((None, D_MODEL, D_MODEL), lay2),
                vec(D_MODEL), vec(D_MODEL),
                dec_spec, dec_spec]
    args = [u, q, k, v, sg, x, state_conv, state_ret,
            conv_w, conv_b, conv_ln_g, conv_ln_b, ret_gn_g, w_out_b, ln1_g, ln1_b, *decay]
    aliases = {}
    if bufs is not None:
        in_specs += [pl.BlockSpec(memory_space=pl.ANY), pl.BlockSpec(memory_space=pl.ANY)]
        args += list(bufs)
        aliases = {18: 1, 19: 2}
    return pl.pallas_call(
        functools.partial(_mix_core_sample_kernel, tb=tb, aliased=bufs is not None),
        grid=(nb // tb,),
        in_specs=in_specs,
        out_specs=[act(D_MODEL), cst_spec, rst_spec],
        out_shape=[jax.ShapeDtypeStruct((nb, D_MODEL), F32),
                   jax.ShapeDtypeStruct(state_conv.shape, F32),
                   jax.ShapeDtypeStruct(state_ret.shape, F32)],
        scratch_shapes=[pltpu.VMEM((tb, CONV_CH), F32)],
        input_output_aliases=aliases,
        compiler_params=pltpu.CompilerParams(dimension_semantics=("arbitrary",),
                                             vmem_limit_bytes=VMEM_LIMIT_BYTES),
        name="mix_core_sample",
    )(*args)


def _ffn_kernel(x_ref, wgu_ref, wdn_ref, g_ref, b_ref, o_ref, a_scr):
    x = x_ref[...]
    xb = x.astype(BF16)
    for s, w in FF_CHUNKS:
        gate = _dot(xb, wgu_ref[:, s:s + w])
        up = _dot(xb, wgu_ref[:, D_FF + s:D_FF + s + w])
        a_scr[:, s:s + w] = (_silu(gate) * up).astype(BF16)
    f = _dot(a_scr[...], wdn_ref[...])
    o_ref[...] = _layer_norm(DN_ALPHA * x + f, g_ref[...], b_ref[...])


def _ffn(x, w_gu_b, w_dn_b, ln2_g, ln2_b, layer, tm):
    m = x.shape[0]
    row = lambda i: (i, 0)
    lay2 = lambda i: (layer, 0, 0)
    once = pl.Buffered(1)
    return pl.pallas_call(
        _ffn_kernel,
        grid=(m // tm,),
        in_specs=[pl.BlockSpec((tm, D_MODEL), row),
                  pl.BlockSpec((None, D_MODEL, 2 * D_FF), lay2, pipeline_mode=once),
                  pl.BlockSpec((None, D_FF, D_MODEL), lay2, pipeline_mode=once),
                  pl.BlockSpec((None, 1, D_MODEL), lay2),
                  pl.BlockSpec((None, 1, D_MODEL), lay2)],
        out_specs=pl.BlockSpec((tm, D_MODEL), row),
        out_shape=jax.ShapeDtypeStruct((m, D_MODEL), F32),
        scratch_shapes=[pltpu.VMEM((tm, D_FF), BF16)],
        compiler_params=pltpu.CompilerParams(dimension_semantics=("arbitrary",),
                                             vmem_limit_bytes=VMEM_LIMIT_BYTES),
        name="ffn",
    )(x, w_gu_b, w_dn_b, ln2_g, ln2_b)


def _rotary_tables(pos):
    half = HEAD_DIM // 2
    inv_freq = ROPE_BASE ** (-jnp.arange(0, half, dtype=F32) / half)
    ang = pos.astype(F32)[:, None] * inv_freq[None, :]
    cos, sin = jnp.cos(ang), jnp.sin(ang)
    cos2 = jnp.concatenate([cos, cos], axis=-1)
    sin2 = jnp.concatenate([-sin, sin], axis=-1)
    scale = HEAD_DIM ** -0.5
    return cos2, sin2, cos2 * scale, sin2 * scale


def _decay_tables(chunk):
    log_g = jnp.log(1.0 - 2.0 ** (-5.0 - jnp.arange(HEADS, dtype=F32)))
    idx = jnp.arange(chunk, dtype=F32)
    diff = idx[:, None] - idx[None, :]
    dmask = jnp.where(diff[None] >= 0,
                      jnp.exp(jnp.maximum(diff, 0.0)[None] * log_g[:, None, None]), 0.0)
    q_decay = jnp.exp((idx + 1.0)[None, :] * log_g[:, None])
    k_decay = jnp.exp((chunk - 1.0 - idx)[None, :] * log_g[:, None])
    chunk_decay = jnp.exp(chunk * log_g)
    return dmask, q_decay, k_decay, chunk_decay


def kernel(x_prompt, x_sample, state_conv, state_ret, w_in, conv_w, conv_b, conv_ln_g, conv_ln_b,
           ret_gn_g, w_out, ln1_g, ln1_b, w_gate_up, w_down, ln2_g, ln2_b):
    batch, seq, _ = x_prompt.shape
    nb = x_sample.shape[0]
    w_in_b, w_out_b = w_in.astype(BF16), w_out.astype(BF16)
    w_gu_b, w_dn_b = w_gate_up.astype(BF16), w_down.astype(BF16)
    vec3 = lambda a: a.reshape(DEPTH, 1, a.shape[-1])
    mixer_params = (conv_w, vec3(conv_b), vec3(conv_ln_g), vec3(conv_ln_b), vec3(ret_gn_g),
                    w_out_b, vec3(ln1_g), vec3(ln1_b))
    ln2_g3, ln2_b3 = vec3(ln2_g), vec3(ln2_b)

    tab_p = _rotary_tables(jnp.arange(seq))
    tab_s = tuple(jnp.broadcast_to(t, (nb, HEAD_DIM))
                  for t in _rotary_tables(PAST_LEN + jnp.arange(1)))

    dmask, q_dec, k_dec, c_dec = _decay_tables(CHUNK)
    bc = lambda a: jnp.broadcast_to(a, (HEADS, CHUNK, HEAD_DIM))
    decay_p = (dmask, bc(q_dec[:, :, None]), bc(k_dec[:, :, None]), bc(c_dec[:, None, None]))
    _, q_dec1, k_dec1, c_dec1 = _decay_tables(1)
    bc1 = lambda a: jnp.broadcast_to(a, (HEADS, HEAD_DIM, HEAD_DIM))
    decay_s = (bc1(k_dec1[:, :, None]), bc1(c_dec1[:, None, None]))

    xp = x_prompt.reshape(batch * seq, D_MODEL)
    xs = x_sample.reshape(nb, D_MODEL)
    conv_p, ret_p = [], []
    bufs = None
    for layer in range(DEPTH):
        u, q, k, v, sg = _mix_in(xp, w_in_b, layer, tab_p, tm=512)
        x1, cst, rst = _mix_core_prompt(u, q, k, v, sg, xp, mixer_params, layer, decay_p,
                                        batch, seq, tl=256)
        conv_p.append(cst)
        ret_p.append(rst)
        xp = _ffn(x1, w_gu_b, w_dn_b, ln2_g3, ln2_b3, layer, tm=512)

        u, q, k, v, sg = _mix_in(xs, w_in_b, layer, tab_s, tm=nb)
        x1, cso, rso = _mix_core_sample(u, q, k, v, sg, xs, state_conv, state_ret, mixer_params,
                                        layer, decay_s, bufs, tb=16)
        bufs = (cso, rso)
        xs = _ffn(x1, w_gu_b, w_dn_b, ln2_g3, ln2_b3, layer, tm=nb)

    return (xp.reshape(batch, seq, D_MODEL), xs.reshape(nb, 1, D_MODEL),
            jnp.stack(conv_p, axis=0), jnp.stack(ret_p, axis=0), bufs[0], bufs[1])
```

```python
import functools

import jax
import jax.numpy as jnp
from jax import lax
from jax.experimental import pallas as pl
from jax.experimental.pallas import tpu as pltpu

F32 = jnp.float32
BF16 = jnp.bfloat16

D_MODEL = 1024
DEPTH = 4
CONV_CH = 512
CONV_K = 31
HEADS = 4
HEAD_DIM = 128
RET_W = HEADS * HEAD_DIM
D_FF = 2816
IN_COLS = 2 * CONV_CH + 4 * RET_W
PAST_LEN = 16384
ROPE_BASE = 10000.0
DN_ALPHA = (2 * DEPTH) ** 0.25
LN_EPS = 1e-5
CHUNK = 256

SUBLANES = 8
VMEM_LIMIT_BYTES = 56 * 1024 * 1024

HIST = 32
CONV_OFF = HIST - (CONV_K - 1)
CONV_ROWS = 32
FF_CHUNKS = tuple((s, min(512, D_FF - s)) for s in range(0, D_FF, 512))

PROMPT_TILE = 512
SAMPLE_TILE = 16


def _dot(a, b):
    return jnp.dot(a, b, preferred_element_type=F32)


def _dot_nt(a, b):
    return lax.dot_general(a, b, (((1,), (1,)), ((), ())), preferred_element_type=F32)


def _layer_norm(y, g, b):
    mu = jnp.mean(y, axis=-1, keepdims=True)
    d = y - mu
    var = jnp.mean(d * d, axis=-1, keepdims=True)
    return d * lax.rsqrt(var + LN_EPS) * g + b


def _group_norm(o, g):
    mu = jnp.mean(o, axis=-1, keepdims=True)
    d = o - mu
    var = jnp.mean(d * d, axis=-1, keepdims=True)
    return d * lax.rsqrt(var + LN_EPS) * g


def _silu(x):
    return x * jax.nn.sigmoid(x)


def _rotary(seg, cos, sin_signed):
    return seg * cos + pltpu.roll(seg, HEAD_DIM // 2, 1) * sin_signed


def _shift_copies(win_ref, sh_ref, rows):
    for r in range(1, SUBLANES):
        sh_ref[r - 1, 0:rows, :] = win_ref[r:r + rows, :]


def _conv_block(win_ref, sh_ref, cw_ref, cbias, clg, clb, r0):
    acc = jnp.broadcast_to(cbias, (CONV_ROWS, CONV_CH))
    for j in range(CONV_K):
        a, r = divmod(CONV_OFF + j, SUBLANES)
        lo = r0 + a * SUBLANES
        src = win_ref[lo:lo + CONV_ROWS, :] if r == 0 else sh_ref[r - 1, lo:lo + CONV_ROWS, :]
        acc = acc + src * jnp.concatenate([cw_ref[j]] * (CONV_ROWS // SUBLANES), axis=0)
    return _silu(_layer_norm(acc, clg, clb)).astype(BF16)


def _swiglu_ffn(x1, wgu_ref, wdn_ref, a_scr):
    xb = x1.astype(BF16)
    for s, w in FF_CHUNKS:
        gate = _dot(xb, wgu_ref[:, s:s + w])
        up = _dot(xb, wgu_ref[:, D_FF + s:D_FF + s + w])
        a_scr[:, s:s + w] = (_silu(gate) * up).astype(BF16)
    return _dot(a_scr[...], wdn_ref[...])


def _prompt_mix_kernel(x_ref, w_ref, cq_ref, sq_ref, ck_ref, sk_ref,
                       cw_ref, cb_ref, clg_ref, clb_ref, gn_ref,
                       dmask_ref, qdec_ref, kdec_ref, cdec_ref,
                       cr_ref, convst_ref, retst_ref,
                       ubuf, shbuf, s_scr, q_scr, k_scr, kd_scr, v_scr, sg_scr, *, tm):
    l = pl.program_id(1)

    @pl.when(l == 0)
    def _():
        ubuf[0:HIST, :] = jnp.zeros((HIST, CONV_CH), F32)
        s_scr[...] = jnp.zeros_like(s_scr)

    xb = x_ref[...].astype(BF16)
    n_chunks = tm // CHUNK

    ca = _dot(xb, w_ref[:, 0:CONV_CH])
    cb = _dot(xb, w_ref[:, CONV_CH:2 * CONV_CH])
    ubuf[HIST:HIST + tm, :] = ca * jax.nn.sigmoid(cb)
    _shift_copies(ubuf, shbuf, tm + HIST - SUBLANES)
    clg, clb, cbias = clg_ref[...], clb_ref[...], cb_ref[...]

    def conv_block(rb):
        r0 = rb * CONV_ROWS
        cr_ref[r0:r0 + CONV_ROWS, 0:CONV_CH] = _conv_block(ubuf, shbuf, cw_ref, cbias, clg, clb, r0)

    o = 2 * CONV_CH
    cq, sq, ck, sk = cq_ref[...], sq_ref[...], ck_ref[...], sk_ref[...]

    def proj_q():
        zq = _dot(xb, w_ref[:, o:o + RET_W])
        for h in range(HEADS):
            cols = slice(h * HEAD_DIM, (h + 1) * HEAD_DIM)
            q_scr[:, cols] = _rotary(zq[:, cols], cq, sq).astype(BF16)

    def proj_k():
        zk = _dot(xb, w_ref[:, o + RET_W:o + 2 * RET_W])
        for h in range(HEADS):
            cols = slice(h * HEAD_DIM, (h + 1) * HEAD_DIM)
            kr = _rotary(zk[:, cols], ck, sk)
            k_scr[:, cols] = kr.astype(BF16)
            for c in range(n_chunks):
                rows = slice(c * CHUNK, (c + 1) * CHUNK)
                kd_scr[rows, cols] = (kr[rows, :] * kdec_ref[h]).astype(BF16)

    def proj_v():
        v_scr[...] = _dot(xb, w_ref[:, o + 2 * RET_W:o + 3 * RET_W]).astype(BF16)

    def proj_gate():
        sg_scr[...] = _silu(_dot(xb, w_ref[:, o + 3 * RET_W:o + 4 * RET_W]))

    def retention(c, h):
        rows = slice(c * CHUNK, (c + 1) * CHUNK)
        cols = slice(h * HEAD_DIM, (h + 1) * HEAD_DIM)
        qc = q_scr[rows, cols]
        vc = v_scr[rows, cols]
        s = s_scr[h]
        sc = _dot_nt(qc, k_scr[rows, cols]) * dmask_ref[h]
        out = _dot(sc.astype(BF16), vc) + _dot(qc, s.astype(BF16)) * qdec_ref[h]
        s_scr[h] = s * cdec_ref[h] + _dot(kd_scr[rows, cols].T, vc)
        on = _group_norm(out, gn_ref[:, cols])
        cr_ref[rows, CONV_CH + h * HEAD_DIM:CONV_CH + (h + 1) * HEAD_DIM] = (
            sg_scr[rows, cols] * on).astype(BF16)

    stages = [proj_q, proj_k, proj_v, proj_gate] + [
        functools.partial(retention, c, h) for c in range(n_chunks) for h in range(HEADS)]
    n_blocks = tm // CONV_ROWS
    done = 0
    for p, stage in enumerate(stages):
        stage()
        upto = (p + 1) * n_blocks // len(stages)
        for rb in range(done, upto):
            conv_block(rb)
        done = upto

    ubuf[0:HIST, :] = ubuf[tm:tm + HIST, :]

    @pl.when(l == pl.num_programs(1) - 1)
    def _():
        convst_ref[...] = ubuf[CONV_OFF:HIST, :]
        retst_ref[...] = s_scr[...]


def _prompt_mix(x, w_in_b, tables, params, layer, decay, batch, seq):
    conv_w_rows, conv_b, conv_ln_g, conv_ln_b, ret_gn_g = params
    tm = PROMPT_TILE
    nl = seq // tm
    row = lambda b, l: (b * nl + l, 0)
    lay2 = lambda b, l: (layer, 0, 0)
    vec = lambda w: pl.BlockSpec((None, 1, w), lay2)
    tab_spec = pl.BlockSpec((tm, HEAD_DIM), lambda b, l: (l, 0))
    dec_spec = lambda rows, cols: pl.BlockSpec((HEADS, rows, cols), lambda b, l: (0, 0, 0))
    return pl.pallas_call(
        functools.partial(_prompt_mix_kernel, tm=tm),
        grid=(batch, nl),
        in_specs=[pl.BlockSpec((tm, D_MODEL), row),
                  pl.BlockSpec((None, D_MODEL, IN_COLS), lay2, pipeline_mode=pl.Buffered(1)),
                  tab_spec, tab_spec, tab_spec, tab_spec,
                  pl.BlockSpec((None, CONV_K, SUBLANES, CONV_CH), lambda b, l: (layer, 0, 0, 0)),
                  vec(CONV_CH), vec(CONV_CH), vec(CONV_CH), vec(RET_W),
                  dec_spec(CHUNK, CHUNK), dec_spec(CHUNK, HEAD_DIM), dec_spec(CHUNK, HEAD_DIM),
                  dec_spec(HEAD_DIM, HEAD_DIM)],
        out_specs=[pl.BlockSpec((tm, D_MODEL), row),
                   pl.BlockSpec((None, CONV_K - 1, CONV_CH), lambda b, l: (b, 0, 0)),
                   pl.BlockSpec((None, HEADS, HEAD_DIM, HEAD_DIM), lambda b, l: (b, 0, 0, 0))],
        out_shape=[jax.ShapeDtypeStruct((batch * seq, D_MODEL), BF16),
                   jax.ShapeDtypeStruct((batch, CONV_K - 1, CONV_CH), F32),
                   jax.ShapeDtypeStruct((batch, HEADS, HEAD_DIM, HEAD_DIM), F32)],
        scratch_shapes=[pltpu.VMEM((HIST + tm, CONV_CH), F32),
                        pltpu.VMEM((SUBLANES - 1, HIST + tm - SUBLANES, CONV_CH), F32),
                        pltpu.VMEM((HEADS, HEAD_DIM, HEAD_DIM), F32),
                        pltpu.VMEM((tm, RET_W), BF16),
                        pltpu.VMEM((tm, RET_W), BF16),
                        pltpu.VMEM((tm, RET_W), BF16),
                        pltpu.VMEM((tm, RET_W), BF16),
                        pltpu.VMEM((tm, RET_W), F32)],
        compiler_params=pltpu.CompilerParams(dimension_semantics=("arbitrary", "arbitrary"),
                                             vmem_limit_bytes=VMEM_LIMIT_BYTES),
        name="prompt_mix",
    )(x, w_in_b, *tables, conv_w_rows, conv_b, conv_ln_g, conv_ln_b, ret_gn_g, *decay)


def _prompt_ffn_kernel(cr_ref, x_ref, wout_ref, l1g_ref, l1b_ref, wgu_ref, wdn_ref, l2g_ref, l2b_ref,
                       o_ref, a_scr, *, tm):
    pm = tm // 2
    rows = (slice(0, pm), slice(pm, tm))
    l1g, l1b, l2g, l2b = l1g_ref[...], l1b_ref[...], l2g_ref[...], l2b_ref[...]

    def out_proj(p):
        return _dot(cr_ref[rows[p], :], wout_ref[...])

    def norm1(p, m):
        return _layer_norm(DN_ALPHA * x_ref[rows[p], :] + m, l1g, l1b)

    def ff_chunk(p, xb, k):
        s, w = FF_CHUNKS[k]
        gate = _dot(xb, wgu_ref[:, s:s + w])
        up = _dot(xb, wgu_ref[:, D_FF + s:D_FF + s + w])
        a_scr[rows[p], s:s + w] = (_silu(gate) * up).astype(BF16)

    def down(p):
        return _dot(a_scr[rows[p], :], wdn_ref[...])

    def norm2(p, x1, f):
        o_ref[rows[p], :] = _layer_norm(DN_ALPHA * x1 + f, l2g, l2b)

    n_ff = len(FF_CHUNKS)
    m0 = out_proj(0)
    m1 = out_proj(1)
    x1_0 = norm1(0, m0)
    xb0 = x1_0.astype(BF16)
    ff_chunk(0, xb0, 0)
    x1_1 = norm1(1, m1)
    xb1 = x1_1.astype(BF16)
    for k in range(1, n_ff):
        ff_chunk(0, xb0, k)
    f0 = down(0)
    ff_chunk(1, xb1, 0)
    norm2(0, x1_0, f0)
    for k in range(1, n_ff):
        ff_chunk(1, xb1, k)
    norm2(1, x1_1, down(1))


def _prompt_ffn(cr, x, w_out_b, ln1_g, ln1_b, w_gu_b, w_dn_b, ln2_g, ln2_b, layer):
    m = x.shape[0]
    tm = PROMPT_TILE
    row = lambda i: (i, 0)
    lay2 = lambda i: (layer, 0, 0)
    once = pl.Buffered(1)
    vec = pl.BlockSpec((None, 1, D_MODEL), lay2)
    return pl.pallas_call(
        functools.partial(_prompt_ffn_kernel, tm=tm),
        grid=(m // tm,),
        in_specs=[pl.BlockSpec((tm, D_MODEL), row),
                  pl.BlockSpec((tm, D_MODEL), row),
                  pl.BlockSpec((None, D_MODEL, D_MODEL), lay2, pipeline_mode=once),
                  vec, vec,
                  pl.BlockSpec((None, D_MODEL, 2 * D_FF), lay2, pipeline_mode=once),
                  pl.BlockSpec((None, D_FF, D_MODEL), lay2, pipeline_mode=once),
                  vec, vec],
        out_specs=pl.BlockSpec((tm, D_MODEL), row),
        out_shape=jax.ShapeDtypeStruct((m, D_MODEL), F32),
        scratch_shapes=[pltpu.VMEM((tm, D_FF), BF16)],
        compiler_params=pltpu.CompilerParams(dimension_semantics=("arbitrary",),
                                             vmem_limit_bytes=VMEM_LIMIT_BYTES),
        name="prompt_ffn",
    )(cr, x, w_out_b, ln1_g, ln1_b, w_gu_b, w_dn_b, ln2_g, ln2_b)


def _mix_in_kernel(x_ref, w_ref, cq_ref, sq_ref, ck_ref, sk_ref,
                   u_ref, q_ref, k_ref, v_ref, sg_ref):
    xb = x_ref[...].astype(BF16)
    ca = _dot(xb, w_ref[:, 0:CONV_CH])
    cb = _dot(xb, w_ref[:, CONV_CH:2 * CONV_CH])
    u_ref[...] = ca * jax.nn.sigmoid(cb)
    o = 2 * CONV_CH
    zq = _dot(xb, w_ref[:, o:o + RET_W])
    zk = _dot(xb, w_ref[:, o + RET_W:o + 2 * RET_W])
    cq, sq, ck, sk = cq_ref[...], sq_ref[...], ck_ref[...], sk_ref[...]
    for h in range(HEADS):
        sl = slice(h * HEAD_DIM, (h + 1) * HEAD_DIM)
        q_ref[:, sl] = _rotary(zq[:, sl], cq, sq).astype(BF16)
        k_ref[:, sl] = _rotary(zk[:, sl], ck, sk)
    v_ref[...] = _dot(xb, w_ref[:, o + 2 * RET_W:o + 3 * RET_W]).astype(BF16)
    sg_ref[...] = _silu(_dot(xb, w_ref[:, o + 3 * RET_W:o + 4 * RET_W]))


def _mix_in(x, w_in_b, layer, tables):
    m = x.shape[0]
    full = lambda w: pl.BlockSpec((m, w), lambda i: (0, 0))
    return pl.pallas_call(
        _mix_in_kernel,
        grid=(1,),
        in_specs=[full(D_MODEL),
                  pl.BlockSpec((None, D_MODEL, IN_COLS), lambda i: (layer, 0, 0)),
                  full(HEAD_DIM), full(HEAD_DIM), full(HEAD_DIM), full(HEAD_DIM)],
        out_specs=[full(RET_W)] * 5,
        out_shape=[jax.ShapeDtypeStruct((m, CONV_CH), F32),
                   jax.ShapeDtypeStruct((m, RET_W), BF16),
                   jax.ShapeDtypeStruct((m, RET_W), F32),
                   jax.ShapeDtypeStruct((m, RET_W), BF16),
                   jax.ShapeDtypeStruct((m, RET_W), F32)],
        compiler_params=pltpu.CompilerParams(dimension_semantics=("arbitrary",),
                                             vmem_limit_bytes=VMEM_LIMIT_BYTES),
        name="mix_in",
    )(x, w_in_b, *tables)


def _mix_core_sample_kernel(*refs, tb, aliased):
    if aliased:
        refs = refs[:18] + refs[20:]
    (u_ref, q_ref, k_ref, v_ref, sg_ref, x_ref, cst_ref, rst_ref,
     cw_ref, cb_ref, clg_ref, clb_ref, gn_ref, wout_ref, l1g_ref, l1b_ref,
     kdec_ref, cdec_ref,
     x1_ref, cso_ref, rso_ref,
     o_scr) = refs

    u = u_ref[...]
    conv = u * cw_ref[CONV_K - 1:CONV_K, :] + cb_ref[...]
    for j in range(CONV_K - 1):
        conv = conv + cst_ref[j] * cw_ref[j:j + 1, :]
        if j > 0:
            cso_ref[j - 1] = cst_ref[j]
    cso_ref[CONV_K - 2] = u
    c = _silu(_layer_norm(conv, clg_ref[...], clb_ref[...])).astype(BF16)

    wide = tb * HEAD_DIM
    in_block = (lax.broadcasted_iota(jnp.int32, (tb, wide), 1) // HEAD_DIM
                == lax.broadcasted_iota(jnp.int32, (tb, wide), 0))
    ones_bd = jnp.where(in_block, 1.0, 0.0).astype(BF16)
    contract_rows = (((0,), (0,)), ((), ()))
    for h in range(HEADS):
        cols = slice(h * HEAD_DIM, (h + 1) * HEAD_DIM)
        kd = (k_ref[:, cols] * kdec_ref[h, 0:tb, :]).astype(BF16)
        v_bd = jnp.where(in_block, jnp.concatenate([v_ref[:, cols].astype(F32)] * tb, axis=1),
                         0.0).astype(BF16)
        kv = lax.dot_general(kd, v_bd, contract_rows, preferred_element_type=F32)
        q_cols = lax.dot_general(q_ref[:, cols], ones_bd, contract_rows,
                                 preferred_element_type=F32)
        for b in range(tb):
            blk = slice(b * HEAD_DIM, (b + 1) * HEAD_DIM)
            s_new = rst_ref[b, h] * cdec_ref[h] + kv[:, blk]
            rso_ref[b, h] = s_new
            o_scr[b:b + 1, cols] = jnp.sum(q_cols[:, blk] * s_new, axis=0, keepdims=True)

    rs = []
    for h in range(HEADS):
        cols = slice(h * HEAD_DIM, (h + 1) * HEAD_DIM)
        on = _group_norm(o_scr[:, cols], gn_ref[:, cols])
        rs.append((sg_ref[:, cols] * on).astype(BF16))
    cr = jnp.concatenate([c] + rs, axis=-1)
    m = _dot(cr, wout_ref[...])
    x1_ref[...] = _layer_norm(DN_ALPHA * x_ref[...] + m, l1g_ref[...], l1b_ref[...])


def _mix_core_sample(u, q, k, v, sg, x, state_conv, state_ret, params, layer, decay, bufs):
    conv_w, conv_b, conv_ln_g, conv_ln_b, ret_gn_g, w_out_b, ln1_g, ln1_b = params
    nb = x.shape[0]
    tb = SAMPLE_TILE
    row = lambda i: (i, 0)
    lay2 = lambda i: (layer, 0, 0)
    act = lambda w: pl.BlockSpec((tb, w), row)
    full = lambda w: pl.BlockSpec((nb, w), lambda i: (0, 0))
    vec = lambda w: pl.BlockSpec((None, 1, w), lay2)
    dec_spec = pl.BlockSpec((HEADS, HEAD_DIM, HEAD_DIM), lambda i: (0, 0, 0))
    cst_spec = pl.BlockSpec((None, CONV_K - 1, tb, CONV_CH), lambda i: (layer, 0, i, 0))
    rst_spec = pl.BlockSpec((None, tb, HEADS, HEAD_DIM, HEAD_DIM), lambda i: (layer, i, 0, 0, 0))
    in_specs = [act(CONV_CH), act(RET_W), act(RET_W), act(RET_W), act(RET_W), act(D_MODEL),
                cst_spec, rst_spec,
                pl.BlockSpec((None, CONV_K, CONV_CH), lay2),
                vec(CONV_CH), vec(CONV_CH), vec(CONV_CH), vec(RET_W),
                pl.BlockSpec((None, D_MODEL, D_MODEL), lay2),
                vec(D_MODEL), vec(D_MODEL),
                dec_spec, dec_spec]
    args = [u, q, k, v, sg, x, state_conv, state_ret,
            conv_w, conv_b, conv_ln_g, conv_ln_b, ret_gn_g, w_out_b, ln1_g, ln1_b, *decay]
    aliases = {}
    if bufs is not None:
        in_specs += [pl.BlockSpec(memory_space=pl.ANY), pl.BlockSpec(memory_space=pl.ANY)]
        args += list(bufs)
        aliases = {18: 1, 19: 2}
    return pl.pallas_call(
        functools.partial(_mix_core_sample_kernel, tb=tb, aliased=bufs is not None),
        grid=(nb // tb,),
        in_specs=in_specs,
        out_specs=[act(D_MODEL), cst_spec, rst_spec],
        out_shape=[jax.ShapeDtypeStruct((nb, D_MODEL), F32),
                   jax.ShapeDtypeStruct(state_conv.shape, F32),
                   jax.ShapeDtypeStruct(state_ret.shape, F32)],
        scratch_shapes=[pltpu.VMEM((tb, RET_W), F32)],
        input_output_aliases=aliases,
        compiler_params=pltpu.CompilerParams(dimension_semantics=("arbitrary",),
                                             vmem_limit_bytes=VMEM_LIMIT_BYTES),
        name="mix_core_sample",
    )(*args)


def _ffn_kernel(x_ref, wgu_ref, wdn_ref, g_ref, b_ref, o_ref, a_scr):
    x = x_ref[...]
    f = _swiglu_ffn(x, wgu_ref, wdn_ref, a_scr)
    o_ref[...] = _layer_norm(DN_ALPHA * x + f, g_ref[...], b_ref[...])


def _ffn(x, w_gu_b, w_dn_b, ln2_g, ln2_b, layer):
    m = x.shape[0]
    lay2 = lambda i: (layer, 0, 0)
    once = pl.Buffered(1)
    return pl.pallas_call(
        _ffn_kernel,
        grid=(1,),
        in_specs=[pl.BlockSpec((m, D_MODEL), lambda i: (0, 0)),
                  pl.BlockSpec((None, D_MODEL, 2 * D_FF), lay2, pipeline_mode=once),
                  pl.BlockSpec((None, D_FF, D_MODEL), lay2, pipeline_mode=once),
                  pl.BlockSpec((None, 1, D_MODEL), lay2),
                  pl.BlockSpec((None, 1, D_MODEL), lay2)],
        out_specs=pl.BlockSpec((m, D_MODEL), lambda i: (0, 0)),
        out_shape=jax.ShapeDtypeStruct((m, D_MODEL), F32),
        scratch_shapes=[pltpu.VMEM((m, D_FF), BF16)],
        compiler_params=pltpu.CompilerParams(dimension_semantics=("arbitrary",),
                                             vmem_limit_bytes=VMEM_LIMIT_BYTES),
        name="ffn",
    )(x, w_gu_b, w_dn_b, ln2_g, ln2_b)


def _rotary_tables(pos):
    half = HEAD_DIM // 2
    inv_freq = ROPE_BASE ** (-jnp.arange(0, half, dtype=F32) / half)
    ang = pos.astype(F32)[:, None] * inv_freq[None, :]
    cos, sin = jnp.cos(ang), jnp.sin(ang)
    cos2 = jnp.concatenate([cos, cos], axis=-1)
    sin2 = jnp.concatenate([-sin, sin], axis=-1)
    scale = HEAD_DIM ** -0.5
    return cos2, sin2, cos2 * scale, sin2 * scale


def _decay_tables(chunk):
    log_g = jnp.log(1.0 - 2.0 ** (-5.0 - jnp.arange(HEADS, dtype=F32)))
    idx = jnp.arange(chunk, dtype=F32)
    diff = idx[:, None] - idx[None, :]
    dmask = jnp.where(diff[None] >= 0,
                      jnp.exp(jnp.maximum(diff, 0.0)[None] * log_g[:, None, None]), 0.0)
    q_decay = jnp.exp((idx + 1.0)[None, :] * log_g[:, None])
    k_decay = jnp.exp((chunk - 1.0 - idx)[None, :] * log_g[:, None])
    chunk_decay = jnp.exp(chunk * log_g)
    return dmask, q_decay, k_decay, chunk_decay


def kernel(x_prompt, x_sample, state_conv, state_ret, w_in, conv_w, conv_b, conv_ln_g, conv_ln_b,
           ret_gn_g, w_out, ln1_g, ln1_b, w_gate_up, w_down, ln2_g, ln2_b):
    batch, seq, _ = x_prompt.shape
    nb = x_sample.shape[0]
    w_in_b, w_out_b = w_in.astype(BF16), w_out.astype(BF16)
    w_gu_b, w_dn_b = w_gate_up.astype(BF16), w_down.astype(BF16)
    vec3 = lambda a: a.reshape(DEPTH, 1, a.shape[-1])
    conv_params = (conv_w, vec3(conv_b), vec3(conv_ln_g), vec3(conv_ln_b), vec3(ret_gn_g))
    conv_w_rows = jnp.broadcast_to(conv_w[:, :, None, :], (DEPTH, CONV_K, SUBLANES, CONV_CH))
    ln1_g3, ln1_b3, ln2_g3, ln2_b3 = vec3(ln1_g), vec3(ln1_b), vec3(ln2_g), vec3(ln2_b)
    sample_params = conv_params + (w_out_b, ln1_g3, ln1_b3)

    tab_p = _rotary_tables(jnp.arange(seq))
    tab_s = tuple(jnp.broadcast_to(t, (nb, HEAD_DIM))
                  for t in _rotary_tables(PAST_LEN + jnp.arange(1)))

    dmask, q_dec, k_dec, c_dec = _decay_tables(CHUNK)
    bc = lambda a: jnp.broadcast_to(a, (HEADS, CHUNK, HEAD_DIM))
    bc1 = lambda a: jnp.broadcast_to(a, (HEADS, HEAD_DIM, HEAD_DIM))
    decay_p = (dmask, bc(q_dec[:, :, None]), bc(k_dec[:, :, None]), bc1(c_dec[:, None, None]))
    _, _, k_dec1, c_dec1 = _decay_tables(1)
    decay_s = (bc1(k_dec1[:, :, None]), bc1(c_dec1[:, None, None]))

    xp = x_prompt.reshape(batch * seq, D_MODEL)
    xs = x_sample.reshape(nb, D_MODEL)
    conv_p, ret_p = [], []
    bufs = None
    state_conv_t = jnp.transpose(state_conv, (0, 2, 1, 3))
    for layer in range(DEPTH):
        cr, cst, rst = _prompt_mix(xp, w_in_b, tab_p, (conv_w_rows,) + conv_params[1:], layer,
                                   decay_p, batch, seq)
        conv_p.append(cst)
        ret_p.append(rst)
        xp = _prompt_ffn(cr, xp, w_out_b, ln1_g3, ln1_b3, w_gu_b, w_dn_b, ln2_g3, ln2_b3, layer)

        u, q, k, v, sg = _mix_in(xs, w_in_b, layer, tab_s)
        x1, cso, rso = _mix_core_sample(u, q, k, v, sg, xs, state_conv_t, state_ret, sample_params,
                                        layer, decay_s, bufs)
        bufs = (cso, rso)
        xs = _ffn(x1, w_gu_b, w_dn_b, ln2_g3, ln2_b3, layer)

    return (xp.reshape(batch, seq, D_MODEL), xs.reshape(nb, 1, D_MODEL),
            jnp.stack(conv_p, axis=0), jnp.stack(ret_p, axis=0),
            jnp.transpose(bufs[0], (0, 2, 1, 3)), bufs[1])
```

```python
import functools

import jax
import jax.numpy as jnp
from jax import lax
from jax.experimental import pallas as pl
from jax.experimental.pallas import tpu as pltpu

F32 = jnp.float32
BF16 = jnp.bfloat16

D_MODEL = 1024
DEPTH = 4
CONV_CH = 512
CONV_K = 31
HEADS = 4
HEAD_DIM = 128
RET_W = HEADS * HEAD_DIM
D_FF = 2816
IN_COLS = 2 * CONV_CH + 4 * RET_W
PAST_LEN = 16384
ROPE_BASE = 10000.0
DN_ALPHA = (2 * DEPTH) ** 0.25
LN_EPS = 1e-5
CHUNK = 256

SUBLANES = 8
VMEM_LIMIT_BYTES = 56 * 1024 * 1024

HIST = 32
CONV_OFF = HIST - (CONV_K - 1)
CONV_ROWS = 64
FF_CHUNKS = tuple((s, min(512, D_FF - s)) for s in range(0, D_FF, 512))

PROMPT_TILE = 512
FFN_TILE = 512
SAMPLE_TILE = 16


def _dot(a, b):
    return jnp.dot(a, b, preferred_element_type=F32)


def _dot_nt(a, b):
    return lax.dot_general(a, b, (((1,), (1,)), ((), ())), preferred_element_type=F32)


def _layer_norm(y, g, b):
    mu = jnp.mean(y, axis=-1, keepdims=True)
    d = y - mu
    var = jnp.mean(d * d, axis=-1, keepdims=True)
    return d * lax.rsqrt(var + LN_EPS) * g + b


def _group_norm(o, g):
    mu = jnp.mean(o, axis=-1, keepdims=True)
    d = o - mu
    var = jnp.mean(d * d, axis=-1, keepdims=True)
    return d * lax.rsqrt(var + LN_EPS) * g


def _silu(x):
    return x * jax.nn.sigmoid(x)


def _rotary(seg, cos, sin_signed):
    return seg * cos + pltpu.roll(seg, HEAD_DIM // 2, 1) * sin_signed


def _shift_copies(win_ref, sh_ref, lo, hi):
    for r in range(1, SUBLANES):
        sh_ref[r - 1, lo:hi, :] = win_ref[r + lo:r + hi, :]


def _conv_block(win_ref, sh_ref, cw_ref, cbias, clg, clb, r0):
    acc = jnp.broadcast_to(cbias, (CONV_ROWS, CONV_CH))
    for j in range(CONV_K):
        a, r = divmod(CONV_OFF + j, SUBLANES)
        lo = r0 + a * SUBLANES
        src = win_ref[lo:lo + CONV_ROWS, :] if r == 0 else sh_ref[r - 1, lo:lo + CONV_ROWS, :]
        acc = acc + src * jnp.concatenate([cw_ref[j]] * (CONV_ROWS // SUBLANES), axis=0)
    return _silu(_layer_norm(acc, clg, clb)).astype(BF16)


def _swiglu_ffn(x1, wgu_ref, wdn_ref, a_scr):
    xb = x1.astype(BF16)
    for s, w in FF_CHUNKS:
        gate = _dot(xb, wgu_ref[:, s:s + w])
        up = _dot(xb, wgu_ref[:, D_FF + s:D_FF + s + w])
        a_scr[:, s:s + w] = (_silu(gate) * up).astype(BF16)
    return _dot(a_scr[...], wdn_ref[...])


def _prompt_mix_kernel(x_ref, w_ref, cq_ref, sq_ref, ck_ref, sk_ref,
                       cw_ref, cb_ref, clg_ref, clb_ref, gn_ref,
                       dmask_ref, qdec_ref, kdec_ref, cdec_ref,
                       cr_ref, convst_ref, retst_ref,
                       ubuf, shbuf, s_scr, q_scr, k_scr, kd_scr, v_scr, sg_scr, *, tm):
    l = pl.program_id(1)

    @pl.when(l == 0)
    def _():
        ubuf[0:HIST, :] = jnp.zeros((HIST, CONV_CH), F32)
        s_scr[...] = jnp.zeros_like(s_scr)

    xb = x_ref[...].astype(BF16)
    n_chunks = tm // CHUNK

    half = tm // 2
    halo = HIST - SUBLANES

    def glu(p):
        rows = slice(p * half, (p + 1) * half)
        ca = _dot(xb[rows, :], w_ref[:, 0:CONV_CH])
        cb = _dot(xb[rows, :], w_ref[:, CONV_CH:2 * CONV_CH])
        ubuf[HIST + p * half:HIST + (p + 1) * half, :] = ca * jax.nn.sigmoid(cb)
        _shift_copies(ubuf, shbuf, p * half + (halo if p else 0), (p + 1) * half + halo)

    clg, clb, cbias = clg_ref[...], clb_ref[...], cb_ref[...]

    def conv_block(rb):
        r0 = rb * CONV_ROWS
        cr_ref[r0:r0 + CONV_ROWS, 0:CONV_CH] = _conv_block(ubuf, shbuf, cw_ref, cbias, clg, clb, r0)

    o = 2 * CONV_CH
    cq, sq, ck, sk = cq_ref[...], sq_ref[...], ck_ref[...], sk_ref[...]

    def proj_q():
        zq = _dot(xb, w_ref[:, o:o + RET_W])
        for h in range(HEADS):
            cols = slice(h * HEAD_DIM, (h + 1) * HEAD_DIM)
            q_scr[:, cols] = _rotary(zq[:, cols], cq, sq).astype(BF16)

    def proj_k():
        zk = _dot(xb, w_ref[:, o + RET_W:o + 2 * RET_W])
        for h in range(HEADS):
            cols = slice(h * HEAD_DIM, (h + 1) * HEAD_DIM)
            kr = _rotary(zk[:, cols], ck, sk)
            k_scr[:, cols] = kr.astype(BF16)
            for c in range(n_chunks):
                rows = slice(c * CHUNK, (c + 1) * CHUNK)
                kd_scr[rows, cols] = (kr[rows, :] * kdec_ref[h]).astype(BF16)

    def proj_v():
        v_scr[...] = _dot(xb, w_ref[:, o + 2 * RET_W:o + 3 * RET_W]).astype(BF16)

    def proj_gate():
        sg_scr[...] = _silu(_dot(xb, w_ref[:, o + 3 * RET_W:o + 4 * RET_W]))

    def retention(c, h):
        rows = slice(c * CHUNK, (c + 1) * CHUNK)
        cols = slice(h * HEAD_DIM, (h + 1) * HEAD_DIM)
        qc = q_scr[rows, cols]
        vc = v_scr[rows, cols]
        s = s_scr[h]
        sc = _dot_nt(qc, k_scr[rows, cols]) * dmask_ref[h]
        out = _dot(sc.astype(BF16), vc) + _dot(qc, s.astype(BF16)) * qdec_ref[h]
        s_scr[h] = s * cdec_ref[h] + _dot(kd_scr[rows, cols].T, vc)
        on = _group_norm(out, gn_ref[:, cols])
        cr_ref[rows, CONV_CH + h * HEAD_DIM:CONV_CH + (h + 1) * HEAD_DIM] = (
            sg_scr[rows, cols] * on).astype(BF16)

    glu(0)
    glu(1)
    stages = [proj_q, proj_k, proj_v, proj_gate] + [
        functools.partial(retention, c, h) for c in range(n_chunks) for h in range(HEADS)]
    n_blocks = tm // CONV_ROWS
    done = 0
    for p, stage in enumerate(stages):
        stage()
        upto = (p + 1) * n_blocks // len(stages)
        for rb in range(done, upto):
            conv_block(rb)
        done = upto

    ubuf[0:HIST, :] = ubuf[tm:tm + HIST, :]

    @pl.when(l == pl.num_programs(1) - 1)
    def _():
        convst_ref[...] = ubuf[CONV_OFF:HIST, :]
        retst_ref[...] = s_scr[...]


def _prompt_mix(x, w_in_b, tables, params, layer, decay, batch, seq):
    conv_w_rows, conv_b, conv_ln_g, conv_ln_b, ret_gn_g = params
    tm = PROMPT_TILE
    nl = seq // tm
    row = lambda b, l: (b * nl + l, 0)
    lay2 = lambda b, l: (layer, 0, 0)
    vec = lambda w: pl.BlockSpec((None, 1, w), lay2)
    tab_spec = pl.BlockSpec((tm, HEAD_DIM), lambda b, l: (l, 0))
    dec_spec = lambda rows, cols: pl.BlockSpec((HEADS, rows, cols), lambda b, l: (0, 0, 0))
    return pl.pallas_call(
        functools.partial(_prompt_mix_kernel, tm=tm),
        grid=(batch, nl),
        in_specs=[pl.BlockSpec((tm, D_MODEL), row),
                  pl.BlockSpec((None, D_MODEL, IN_COLS), lambda b, l: (0, 0, 0),
                               pipeline_mode=pl.Buffered(1)),
                  tab_spec, tab_spec, tab_spec, tab_spec,
                  pl.BlockSpec((None, CONV_K, SUBLANES, CONV_CH), lambda b, l: (layer, 0, 0, 0)),
                  vec(CONV_CH), vec(CONV_CH), vec(CONV_CH), vec(RET_W),
                  dec_spec(CHUNK, CHUNK), dec_spec(CHUNK, HEAD_DIM), dec_spec(CHUNK, HEAD_DIM),
                  dec_spec(HEAD_DIM, HEAD_DIM)],
        out_specs=[pl.BlockSpec((tm, D_MODEL), row),
                   pl.BlockSpec((None, CONV_K - 1, CONV_CH), lambda b, l: (b, 0, 0)),
                   pl.BlockSpec((None, HEADS, HEAD_DIM, HEAD_DIM), lambda b, l: (b, 0, 0, 0))],
        out_shape=[jax.ShapeDtypeStruct((batch * seq, D_MODEL), BF16),
                   jax.ShapeDtypeStruct((batch, CONV_K - 1, CONV_CH), F32),
                   jax.ShapeDtypeStruct((batch, HEADS, HEAD_DIM, HEAD_DIM), F32)],
        scratch_shapes=[pltpu.VMEM((HIST + tm, CONV_CH), F32),
                        pltpu.VMEM((SUBLANES - 1, HIST + tm - SUBLANES, CONV_CH), F32),
                        pltpu.VMEM((HEADS, HEAD_DIM, HEAD_DIM), F32),
                        pltpu.VMEM((tm, RET_W), BF16),
                        pltpu.VMEM((tm, RET_W), BF16),
                        pltpu.VMEM((tm, RET_W), BF16),
                        pltpu.VMEM((tm, RET_W), BF16),
                        pltpu.VMEM((tm, RET_W), F32)],
        compiler_params=pltpu.CompilerParams(dimension_semantics=("arbitrary", "arbitrary"),
                                             vmem_limit_bytes=VMEM_LIMIT_BYTES),
        name="prompt_mix",
    )(x, w_in_b, *tables, conv_w_rows, conv_b, conv_ln_g, conv_ln_b, ret_gn_g, *decay)


def _prompt_ffn_kernel(cr_ref, x_ref, wout_ref, l1g_ref, l1b_ref, wgu_ref, wdn_ref, l2g_ref, l2b_ref,
                       *rest, tm):
    n_cast = (len(rest) - 2) // 2
    o_ref, a_scr = rest[n_cast], rest[-1]
    for src_ref, dst_ref in zip(rest[:n_cast], rest[n_cast + 1:-1]):
        dst_ref[...] = src_ref[...].astype(BF16)

    pm = tm // 2
    rows = (slice(0, pm), slice(pm, tm))
    l1g, l1b, l2g, l2b = l1g_ref[...], l1b_ref[...], l2g_ref[...], l2b_ref[...]

    def out_proj(p):
        return _dot(cr_ref[rows[p], :], wout_ref[...])

    def norm1(p, m):
        return _layer_norm(DN_ALPHA * x_ref[rows[p], :] + m, l1g, l1b)

    def ff_chunk(p, xb, k):
        s, w = FF_CHUNKS[k]
        gate = _dot(xb, wgu_ref[:, s:s + w])
        up = _dot(xb, wgu_ref[:, D_FF + s:D_FF + s + w])
        a_scr[rows[p], s:s + w] = (_silu(gate) * up).astype(BF16)

    def down(p):
        return _dot(a_scr[rows[p], :], wdn_ref[...])

    def norm2(p, x1, f):
        o_ref[rows[p], :] = _layer_norm(DN_ALPHA * x1 + f, l2g, l2b)

    n_ff = len(FF_CHUNKS)
    m0 = out_proj(0)
    m1 = out_proj(1)
    x1_0 = norm1(0, m0)
    xb0 = x1_0.astype(BF16)
    ff_chunk(0, xb0, 0)
    x1_1 = norm1(1, m1)
    xb1 = x1_1.astype(BF16)
    for k in range(1, n_ff):
        ff_chunk(0, xb0, k)
    f0 = down(0)
    ff_chunk(1, xb1, 0)
    norm2(0, x1_0, f0)
    for k in range(1, n_ff):
        ff_chunk(1, xb1, k)
    norm2(1, x1_1, down(1))


def _prompt_ffn(cr, x, w_out_b, ln1_g, ln1_b, w_gu_b, w_dn_b, ln2_g, ln2_b, layer, next_weights):
    m = x.shape[0]
    tm = FFN_TILE
    n_steps = m // tm
    steps_per_slab = 2
    n_slabs = n_steps // steps_per_slab
    row = lambda i: (i, 0)
    lay2 = lambda i: (layer, 0, 0)
    fixed = lambda i: (0, 0, 0)
    once = pl.Buffered(1)
    vec = pl.BlockSpec((None, 1, D_MODEL), lay2)
    cast_in, cast_out, cast_shapes = [], [], []
    for w in next_weights:
        _, rows, cols = w.shape
        slab = rows // n_slabs
        cast_in.append(pl.BlockSpec((None, slab, cols),
                                    lambda i: (layer + 1, i // steps_per_slab, 0)))
        cast_out.append(pl.BlockSpec((None, slab, cols), lambda i: (0, i // steps_per_slab, 0)))
        cast_shapes.append(jax.ShapeDtypeStruct((1, rows, cols), BF16))
    outs = pl.pallas_call(
        functools.partial(_prompt_ffn_kernel, tm=tm),
        grid=(n_steps,),
        in_specs=[pl.BlockSpec((tm, D_MODEL), row),
                  pl.BlockSpec((tm, D_MODEL), row),
                  pl.BlockSpec((None, D_MODEL, D_MODEL), fixed, pipeline_mode=once),
                  vec, vec,
                  pl.BlockSpec((None, D_MODEL, 2 * D_FF), fixed, pipeline_mode=once),
                  pl.BlockSpec((None, D_FF, D_MODEL), fixed, pipeline_mode=once),
                  vec, vec] + cast_in,
        out_specs=[pl.BlockSpec((tm, D_MODEL), row)] + cast_out,
        out_shape=[jax.ShapeDtypeStruct((m, D_MODEL), F32)] + cast_shapes,
        scratch_shapes=[pltpu.VMEM((tm, D_FF), BF16)],
        compiler_params=pltpu.CompilerParams(dimension_semantics=("arbitrary",),
                                             vmem_limit_bytes=VMEM_LIMIT_BYTES),
        name="prompt_ffn",
    )(cr, x, w_out_b, ln1_g, ln1_b, w_gu_b, w_dn_b, ln2_g, ln2_b, *next_weights)
    return outs[0], tuple(outs[1:])


def _mix_in_kernel(x_ref, w_ref, cq_ref, sq_ref, ck_ref, sk_ref,
                   u_ref, q_ref, k_ref, v_ref, sg_ref):
    xb = x_ref[...].astype(BF16)
    ca = _dot(xb, w_ref[:, 0:CONV_CH])
    cb = _dot(xb, w_ref[:, CONV_CH:2 * CONV_CH])
    u_ref[...] = ca * jax.nn.sigmoid(cb)
    o = 2 * CONV_CH
    zq = _dot(xb, w_ref[:, o:o + RET_W])
    zk = _dot(xb, w_ref[:, o + RET_W:o + 2 * RET_W])
    cq, sq, ck, sk = cq_ref[...], sq_ref[...], ck_ref[...], sk_ref[...]
    for h in range(HEADS):
        sl = slice(h * HEAD_DIM, (h + 1) * HEAD_DIM)
        q_ref[:, sl] = _rotary(zq[:, sl], cq, sq).astype(BF16)
        k_ref[:, sl] = _rotary(zk[:, sl], ck, sk)
    v_ref[...] = _dot(xb, w_ref[:, o + 2 * RET_W:o + 3 * RET_W]).astype(BF16)
    sg_ref[...] = _silu(_dot(xb, w_ref[:, o + 3 * RET_W:o + 4 * RET_W]))


def _mix_in(x, w_in_b, layer, tables):
    m = x.shape[0]
    full = lambda w: pl.BlockSpec((m, w), lambda i: (0, 0))
    return pl.pallas_call(
        _mix_in_kernel,
        grid=(1,),
        in_specs=[full(D_MODEL),
                  pl.BlockSpec((None, D_MODEL, IN_COLS), lambda i: (0, 0, 0)),
                  full(HEAD_DIM), full(HEAD_DIM), full(HEAD_DIM), full(HEAD_DIM)],
        out_specs=[full(RET_W)] * 5,
        out_shape=[jax.ShapeDtypeStruct((m, CONV_CH), F32),
                   jax.ShapeDtypeStruct((m, RET_W), BF16),
                   jax.ShapeDtypeStruct((m, RET_W), F32),
                   jax.ShapeDtypeStruct((m, RET_W), BF16),
                   jax.ShapeDtypeStruct((m, RET_W), F32)],
        compiler_params=pltpu.CompilerParams(dimension_semantics=("arbitrary",),
                                             vmem_limit_bytes=VMEM_LIMIT_BYTES),
        name="mix_in",
    )(x, w_in_b, *tables)


def _mix_core_sample_kernel(*refs, tb, aliased):
    if aliased:
        refs = refs[:18] + refs[20:]
    (u_ref, q_ref, k_ref, v_ref, sg_ref, x_ref, cst_ref, rst_ref,
     cw_ref, cb_ref, clg_ref, clb_ref, gn_ref, wout_ref, l1g_ref, l1b_ref,
     kdec_ref, cdec_ref,
     x1_ref, cso_ref, rso_ref,
     o_scr) = refs

    u = u_ref[...]
    conv = u * cw_ref[CONV_K - 1:CONV_K, :] + cb_ref[...]
    for j in range(CONV_K - 1):
        conv = conv + cst_ref[j] * cw_ref[j:j + 1, :]
        if j > 0:
            cso_ref[j - 1] = cst_ref[j]
    cso_ref[CONV_K - 2] = u
    c = _silu(_layer_norm(conv, clg_ref[...], clb_ref[...])).astype(BF16)

    wide = tb * HEAD_DIM
    in_block = (lax.broadcasted_iota(jnp.int32, (tb, wide), 1) // HEAD_DIM
                == lax.broadcasted_iota(jnp.int32, (tb, wide), 0))
    ones_bd = jnp.where(in_block, 1.0, 0.0).astype(BF16)
    contract_rows = (((0,), (0,)), ((), ()))
    for h in range(HEADS):
        cols = slice(h * HEAD_DIM, (h + 1) * HEAD_DIM)
        kd = (k_ref[:, cols] * kdec_ref[h, 0:tb, :]).astype(BF16)
        v_bd = jnp.where(in_block, jnp.concatenate([v_ref[:, cols].astype(F32)] * tb, axis=1),
                         0.0).astype(BF16)
        kv = lax.dot_general(kd, v_bd, contract_rows, preferred_element_type=F32)
        q_cols = lax.dot_general(q_ref[:, cols], ones_bd, contract_rows,
                                 preferred_element_type=F32)
        for b in range(tb):
            blk = slice(b * HEAD_DIM, (b + 1) * HEAD_DIM)
            s_new = rst_ref[b, h] * cdec_ref[h] + kv[:, blk]
            rso_ref[b, h] = s_new
            o_scr[b:b + 1, cols] = jnp.sum(q_cols[:, blk] * s_new, axis=0, keepdims=True)

    rs = []
    for h in range(HEADS):
        cols = slice(h * HEAD_DIM, (h + 1) * HEAD_DIM)
        on = _group_norm(o_scr[:, cols], gn_ref[:, cols])
        rs.append((sg_ref[:, cols] * on).astype(BF16))
    cr = jnp.concatenate([c] + rs, axis=-1)
    m = _dot(cr, wout_ref[...])
    x1_ref[...] = _layer_norm(DN_ALPHA * x_ref[...] + m, l1g_ref[...], l1b_ref[...])


def _mix_core_sample(u, q, k, v, sg, x, state_conv, state_ret, params, layer, decay, bufs):
    conv_w, conv_b, conv_ln_g, conv_ln_b, ret_gn_g, w_out_b, ln1_g, ln1_b = params
    nb = x.shape[0]
    tb = SAMPLE_TILE
    row = lambda i: (i, 0)
    lay2 = lambda i: (layer, 0, 0)
    act = lambda w: pl.BlockSpec((tb, w), row)
    full = lambda w: pl.BlockSpec((nb, w), lambda i: (0, 0))
    vec = lambda w: pl.BlockSpec((None, 1, w), lay2)
    dec_spec = pl.BlockSpec((HEADS, HEAD_DIM, HEAD_DIM), lambda i: (0, 0, 0))
    cst_spec = pl.BlockSpec((None, CONV_K - 1, tb, CONV_CH), lambda i: (layer, 0, i, 0))
    rst_spec = pl.BlockSpec((None, tb, HEADS, HEAD_DIM, HEAD_DIM), lambda i: (layer, i, 0, 0, 0))
    in_specs = [act(CONV_CH), act(RET_W), act(RET_W), act(RET_W), act(RET_W), act(D_MODEL),
                cst_spec, rst_spec,
                pl.BlockSpec((None, CONV_K, CONV_CH), lay2),
                vec(CONV_CH), vec(CONV_CH), vec(CONV_CH), vec(RET_W),
                pl.BlockSpec((None, D_MODEL, D_MODEL), lambda i: (0, 0, 0)),
                vec(D_MODEL), vec(D_MODEL),
                dec_spec, dec_spec]
    args = [u, q, k, v, sg, x, state_conv, state_ret,
            conv_w, conv_b, conv_ln_g, conv_ln_b, ret_gn_g, w_out_b, ln1_g, ln1_b, *decay]
    aliases = {}
    if bufs is not None:
        in_specs += [pl.BlockSpec(memory_space=pl.ANY), pl.BlockSpec(memory_space=pl.ANY)]
        args += list(bufs)
        aliases = {18: 1, 19: 2}
    return pl.pallas_call(
        functools.partial(_mix_core_sample_kernel, tb=tb, aliased=bufs is not None),
        grid=(nb // tb,),
        in_specs=in_specs,
        out_specs=[act(D_MODEL), cst_spec, rst_spec],
        out_shape=[jax.ShapeDtypeStruct((nb, D_MODEL), F32),
                   jax.ShapeDtypeStruct(state_conv.shape, F32),
                   jax.ShapeDtypeStruct(state_ret.shape, F32)],
        scratch_shapes=[pltpu.VMEM((tb, RET_W), F32)],
        input_output_aliases=aliases,
        compiler_params=pltpu.CompilerParams(dimension_semantics=("arbitrary",),
                                             vmem_limit_bytes=VMEM_LIMIT_BYTES),
        name="mix_core_sample",
    )(*args)


def _ffn_kernel(x_ref, wgu_ref, wdn_ref, g_ref, b_ref, o_ref, a_scr):
    x = x_ref[...]
    f = _swiglu_ffn(x, wgu_ref, wdn_ref, a_scr)
    o_ref[...] = _layer_norm(DN_ALPHA * x + f, g_ref[...], b_ref[...])


def _ffn(x, w_gu_b, w_dn_b, ln2_g, ln2_b, layer):
    m = x.shape[0]
    lay2 = lambda i: (layer, 0, 0)
    once = pl.Buffered(1)
    return pl.pallas_call(
        _ffn_kernel,
        grid=(1,),
        in_specs=[pl.BlockSpec((m, D_MODEL), lambda i: (0, 0)),
                  pl.BlockSpec((None, D_MODEL, 2 * D_FF), lambda i: (0, 0, 0), pipeline_mode=once),
                  pl.BlockSpec((None, D_FF, D_MODEL), lambda i: (0, 0, 0), pipeline_mode=once),
                  pl.BlockSpec((None, 1, D_MODEL), lay2),
                  pl.BlockSpec((None, 1, D_MODEL), lay2)],
        out_specs=pl.BlockSpec((m, D_MODEL), lambda i: (0, 0)),
        out_shape=jax.ShapeDtypeStruct((m, D_MODEL), F32),
        scratch_shapes=[pltpu.VMEM((m, D_FF), BF16)],
        compiler_params=pltpu.CompilerParams(dimension_semantics=("arbitrary",),
                                             vmem_limit_bytes=VMEM_LIMIT_BYTES),
        name="ffn",
    )(x, w_gu_b, w_dn_b, ln2_g, ln2_b)


def _rotary_tables(pos):
    half = HEAD_DIM // 2
    inv_freq = ROPE_BASE ** (-jnp.arange(0, half, dtype=F32) / half)
    ang = pos.astype(F32)[:, None] * inv_freq[None, :]
    cos, sin = jnp.cos(ang), jnp.sin(ang)
    cos2 = jnp.concatenate([cos, cos], axis=-1)
    sin2 = jnp.concatenate([-sin, sin], axis=-1)
    scale = HEAD_DIM ** -0.5
    return cos2, sin2, cos2 * scale, sin2 * scale


def _decay_tables(chunk):
    log_g = jnp.log(1.0 - 2.0 ** (-5.0 - jnp.arange(HEADS, dtype=F32)))
    idx = jnp.arange(chunk, dtype=F32)
    diff = idx[:, None] - idx[None, :]
    dmask = jnp.where(diff[None] >= 0,
                      jnp.exp(jnp.maximum(diff, 0.0)[None] * log_g[:, None, None]), 0.0)
    q_decay = jnp.exp((idx + 1.0)[None, :] * log_g[:, None])
    k_decay = jnp.exp((chunk - 1.0 - idx)[None, :] * log_g[:, None])
    chunk_decay = jnp.exp(chunk * log_g)
    return dmask, q_decay, k_decay, chunk_decay


def kernel(x_prompt, x_sample, state_conv, state_ret, w_in, conv_w, conv_b, conv_ln_g, conv_ln_b,
           ret_gn_g, w_out, ln1_g, ln1_b, w_gate_up, w_down, ln2_g, ln2_b):
    batch, seq, _ = x_prompt.shape
    nb = x_sample.shape[0]
    weights_f32 = (w_in, w_out, w_gate_up, w_down)
    weights_b = tuple(w[0:1].astype(BF16) for w in weights_f32)
    vec3 = lambda a: a.reshape(DEPTH, 1, a.shape[-1])
    conv_params = (conv_w, vec3(conv_b), vec3(conv_ln_g), vec3(conv_ln_b), vec3(ret_gn_g))
    conv_w_rows = jnp.broadcast_to(conv_w[:, :, None, :], (DEPTH, CONV_K, SUBLANES, CONV_CH))
    ln1_g3, ln1_b3, ln2_g3, ln2_b3 = vec3(ln1_g), vec3(ln1_b), vec3(ln2_g), vec3(ln2_b)

    tab_p = _rotary_tables(jnp.arange(seq))
    tab_s = tuple(jnp.broadcast_to(t, (nb, HEAD_DIM))
                  for t in _rotary_tables(PAST_LEN + jnp.arange(1)))

    dmask, q_dec, k_dec, c_dec = _decay_tables(CHUNK)
    bc = lambda a: jnp.broadcast_to(a, (HEADS, CHUNK, HEAD_DIM))
    bc1 = lambda a: jnp.broadcast_to(a, (HEADS, HEAD_DIM, HEAD_DIM))
    decay_p = (dmask, bc(q_dec[:, :, None]), bc(k_dec[:, :, None]), bc1(c_dec[:, None, None]))
    _, _, k_dec1, c_dec1 = _decay_tables(1)
    decay_s = (bc1(k_dec1[:, :, None]), bc1(c_dec1[:, None, None]))

    xp = x_prompt.reshape(batch * seq, D_MODEL)
    xs = x_sample.reshape(nb, D_MODEL)
    conv_p, ret_p = [], []
    bufs = None
    state_conv_t = jnp.transpose(state_conv, (0, 2, 1, 3))
    for layer in range(DEPTH):
        w_in_b, w_out_b, w_gu_b, w_dn_b = weights_b
        cr, cst, rst = _prompt_mix(xp, w_in_b, tab_p, (conv_w_rows,) + conv_params[1:], layer,
                                   decay_p, batch, seq)
        conv_p.append(cst)
        ret_p.append(rst)
        xp, weights_b = _prompt_ffn(cr, xp, w_out_b, ln1_g3, ln1_b3, w_gu_b, w_dn_b, ln2_g3, ln2_b3,
                                    layer, weights_f32 if layer + 1 < DEPTH else ())

        u, q, k, v, sg = _mix_in(xs, w_in_b, layer, tab_s)
        x1, cso, rso = _mix_core_sample(u, q, k, v, sg, xs, state_conv_t, state_ret,
                                        conv_params + (w_out_b, ln1_g3, ln1_b3), layer, decay_s, bufs)
        bufs = (cso, rso)
        xs = _ffn(x1, w_gu_b, w_dn_b, ln2_g3, ln2_b3, layer)

    return (xp.reshape(batch, seq, D_MODEL), xs.reshape(nb, 1, D_MODEL),
            jnp.stack(conv_p, axis=0), jnp.stack(ret_p, axis=0),
            jnp.transpose(bufs[0], (0, 2, 1, 3)), bufs[1])
```

```python
import functools

import jax
import jax.numpy as jnp
from jax import lax
from jax.experimental import pallas as pl
from jax.experimental.pallas import tpu as pltpu

F32 = jnp.float32
BF16 = jnp.bfloat16

D_MODEL = 1024
DEPTH = 4
CONV_CH = 512
CONV_K = 31
HEADS = 4
HEAD_DIM = 128
RET_W = HEADS * HEAD_DIM
D_FF = 2816
IN_COLS = 2 * CONV_CH + 4 * RET_W
PAST_LEN = 16384
ROPE_BASE = 10000.0
DN_ALPHA = (2 * DEPTH) ** 0.25
LN_EPS = 1e-5
CHUNK = 256

SUBLANES = 8
VMEM_LIMIT_BYTES = 56 * 1024 * 1024

HIST = 32
CONV_OFF = HIST - (CONV_K - 1)
CONV_ROWS = 64
FF_CHUNKS = tuple((s, min(256, D_FF - s)) for s in range(0, D_FF, 256))

PROMPT_TILE = 512
FFN_TILE = 512
SAMPLE_TILE = 16


def _dot(a, b):
    return jnp.dot(a, b, preferred_element_type=F32)


def _dot_nt(a, b):
    return lax.dot_general(a, b, (((1,), (1,)), ((), ())), preferred_element_type=F32)


def _layer_norm(y, g, b):
    mu = jnp.mean(y, axis=-1, keepdims=True)
    d = y - mu
    var = jnp.mean(d * d, axis=-1, keepdims=True)
    return d * lax.rsqrt(var + LN_EPS) * g + b


def _group_norm(o, g):
    mu = jnp.mean(o, axis=-1, keepdims=True)
    d = o - mu
    var = jnp.mean(d * d, axis=-1, keepdims=True)
    return d * lax.rsqrt(var + LN_EPS) * g


def _silu(x):
    return x * jax.nn.sigmoid(x)


def _rotary(seg, cos, sin_signed):
    return seg * cos + pltpu.roll(seg, HEAD_DIM // 2, 1) * sin_signed


def _shift_copies(win_ref, sh_ref, lo, hi):
    for r in range(1, SUBLANES):
        sh_ref[r - 1, lo:hi, :] = win_ref[r + lo:r + hi, :]


def _conv_block(win_ref, sh_ref, cw_ref, cbias, clg, clb, r0):
    acc = jnp.broadcast_to(cbias, (CONV_ROWS, CONV_CH))
    for j in range(CONV_K):
        a, r = divmod(CONV_OFF + j, SUBLANES)
        lo = r0 + a * SUBLANES
        src = win_ref[lo:lo + CONV_ROWS, :] if r == 0 else sh_ref[r - 1, lo:lo + CONV_ROWS, :]
        acc = acc + src * jnp.concatenate([cw_ref[j]] * (CONV_ROWS // SUBLANES), axis=0)
    return _silu(_layer_norm(acc, clg, clb)).astype(BF16)


def _swiglu_ffn(x1, wgu_ref, wdn_ref, a_scr):
    xb = x1.astype(BF16)
    for s, w in FF_CHUNKS:
        gate = _dot(xb, wgu_ref[:, s:s + w])
        up = _dot(xb, wgu_ref[:, D_FF + s:D_FF + s + w])
        a_scr[:, s:s + w] = (_silu(gate) * up).astype(BF16)
    return _dot(a_scr[...], wdn_ref[...])


def _prompt_mix_kernel(x_ref, w_ref, cq_ref, sq_ref, ck_ref, sk_ref,
                       cw_ref, cb_ref, clg_ref, clb_ref, gn_ref, wout_ref,
                       dmask_ref, qdec_ref, kdec_ref, cdec_ref,
                       m_ref, convst_ref, retst_ref,
                       ubuf, shbuf, cr_ref, s_scr, q_scr, k_scr, kd_scr, v_scr, sg_scr, *, tm):
    l = pl.program_id(1)

    @pl.when(l == 0)
    def _():
        ubuf[0:HIST, :] = jnp.zeros((HIST, CONV_CH), F32)
        s_scr[...] = jnp.zeros_like(s_scr)

    xb = x_ref[...].astype(BF16)
    n_chunks = tm // CHUNK

    half = tm // 2
    halo = HIST - SUBLANES

    def glu(p):
        rows = slice(p * half, (p + 1) * half)
        ca = _dot(xb[rows, :], w_ref[:, 0:CONV_CH])
        cb = _dot(xb[rows, :], w_ref[:, CONV_CH:2 * CONV_CH])
        ubuf[HIST + p * half:HIST + (p + 1) * half, :] = ca * jax.nn.sigmoid(cb)
        _shift_copies(ubuf, shbuf, p * half + (halo if p else 0), (p + 1) * half + halo)

    clg, clb, cbias = clg_ref[...], clb_ref[...], cb_ref[...]

    def conv_block(rb):
        r0 = rb * CONV_ROWS
        cr_ref[r0:r0 + CONV_ROWS, 0:CONV_CH] = _conv_block(ubuf, shbuf, cw_ref, cbias, clg, clb, r0)

    o = 2 * CONV_CH
    cq, sq, ck, sk = cq_ref[...], sq_ref[...], ck_ref[...], sk_ref[...]

    def proj_q():
        zq = _dot(xb, w_ref[:, o:o + RET_W])
        for h in range(HEADS):
            cols = slice(h * HEAD_DIM, (h + 1) * HEAD_DIM)
            q_scr[:, cols] = _rotary(zq[:, cols], cq, sq).astype(BF16)

    def proj_k():
        zk = _dot(xb, w_ref[:, o + RET_W:o + 2 * RET_W])
        for h in range(HEADS):
            cols = slice(h * HEAD_DIM, (h + 1) * HEAD_DIM)
            kr = _rotary(zk[:, cols], ck, sk)
            k_scr[:, cols] = kr.astype(BF16)
            for c in range(n_chunks):
                rows = slice(c * CHUNK, (c + 1) * CHUNK)
                kd_scr[rows, cols] = (kr[rows, :] * kdec_ref[h]).astype(BF16)

    def proj_v():
        v_scr[...] = _dot(xb, w_ref[:, o + 2 * RET_W:o + 3 * RET_W]).astype(BF16)

    def proj_gate():
        sg_scr[...] = _silu(_dot(xb, w_ref[:, o + 3 * RET_W:o + 4 * RET_W]))

    def retention(c, h):
        rows = slice(c * CHUNK, (c + 1) * CHUNK)
        cols = slice(h * HEAD_DIM, (h + 1) * HEAD_DIM)
        qc = q_scr[rows, cols]
        vc = v_scr[rows, cols]
        s = s_scr[h]
        sc = _dot_nt(qc, k_scr[rows, cols]) * dmask_ref[h]
        out = _dot(sc.astype(BF16), vc) + _dot(qc, s.astype(BF16)) * qdec_ref[h]
        s_scr[h] = s * cdec_ref[h] + _dot(kd_scr[rows, cols].T, vc)
        on = _group_norm(out, gn_ref[:, cols])
        cr_ref[rows, CONV_CH + h * HEAD_DIM:CONV_CH + (h + 1) * HEAD_DIM] = (
            sg_scr[rows, cols] * on).astype(BF16)

    def out_proj(c):
        rows = slice(c * CHUNK, (c + 1) * CHUNK)
        m_ref[rows, :] = _dot(cr_ref[rows, :], wout_ref[...])

    glu(0)
    glu(1)
    stages = [proj_q, proj_k, proj_v, proj_gate]
    proj_stage = []
    for c in range(n_chunks):
        stages += [functools.partial(retention, c, h) for h in range(HEADS)]
        proj_stage.append(len(stages))
        stages.append(functools.partial(out_proj, c))
    n_blocks = tm // CONV_ROWS
    blocks_per_chunk = CHUNK // CONV_ROWS
    assert all((c + 1) * blocks_per_chunk <= proj_stage[c] for c in range(n_chunks))
    for p, stage in enumerate(stages):
        stage()
        if p < n_blocks:
            conv_block(p)

    ubuf[0:HIST, :] = ubuf[tm:tm + HIST, :]

    @pl.when(l == pl.num_programs(1) - 1)
    def _():
        convst_ref[...] = ubuf[CONV_OFF:HIST, :]
        retst_ref[...] = s_scr[...]


def _prompt_mix(x, w_in_b, w_out_b, tables, params, layer, decay, batch, seq):
    conv_w_rows, conv_b, conv_ln_g, conv_ln_b, ret_gn_g = params
    tm = PROMPT_TILE
    nl = seq // tm
    row = lambda b, l: (b * nl + l, 0)
    lay2 = lambda b, l: (layer, 0, 0)
    vec = lambda w: pl.BlockSpec((None, 1, w), lay2)
    tab_spec = pl.BlockSpec((tm, HEAD_DIM), lambda b, l: (l, 0))
    dec_spec = lambda rows, cols: pl.BlockSpec((HEADS, rows, cols), lambda b, l: (0, 0, 0))
    return pl.pallas_call(
        functools.partial(_prompt_mix_kernel, tm=tm),
        grid=(batch, nl),
        in_specs=[pl.BlockSpec((tm, D_MODEL), row),
                  pl.BlockSpec((None, D_MODEL, IN_COLS), lambda b, l: (0, 0, 0),
                               pipeline_mode=pl.Buffered(1)),
                  tab_spec, tab_spec, tab_spec, tab_spec,
                  pl.BlockSpec((None, CONV_K, SUBLANES, CONV_CH), lambda b, l: (layer, 0, 0, 0)),
                  vec(CONV_CH), vec(CONV_CH), vec(CONV_CH), vec(RET_W),
                  pl.BlockSpec((None, D_MODEL, D_MODEL), lambda b, l: (0, 0, 0),
                               pipeline_mode=pl.Buffered(1)),
                  dec_spec(CHUNK, CHUNK), dec_spec(CHUNK, HEAD_DIM), dec_spec(CHUNK, HEAD_DIM),
                  dec_spec(HEAD_DIM, HEAD_DIM)],
        out_specs=[pl.BlockSpec((tm, D_MODEL), row),
                   pl.BlockSpec((None, CONV_K - 1, CONV_CH), lambda b, l: (b, 0, 0)),
                   pl.BlockSpec((None, HEADS, HEAD_DIM, HEAD_DIM), lambda b, l: (b, 0, 0, 0))],
        out_shape=[jax.ShapeDtypeStruct((batch * seq, D_MODEL), F32),
                   jax.ShapeDtypeStruct((batch, CONV_K - 1, CONV_CH), F32),
                   jax.ShapeDtypeStruct((batch, HEADS, HEAD_DIM, HEAD_DIM), F32)],
        scratch_shapes=[pltpu.VMEM((HIST + tm, CONV_CH), F32),
                        pltpu.VMEM((SUBLANES - 1, HIST + tm - SUBLANES, CONV_CH), F32),
                        pltpu.VMEM((tm, D_MODEL), BF16),
                        pltpu.VMEM((HEADS, HEAD_DIM, HEAD_DIM), F32),
                        pltpu.VMEM((tm, RET_W), BF16),
                        pltpu.VMEM((tm, RET_W), BF16),
                        pltpu.VMEM((tm, RET_W), BF16),
                        pltpu.VMEM((tm, RET_W), BF16),
                        pltpu.VMEM((tm, RET_W), F32)],
        compiler_params=pltpu.CompilerParams(dimension_semantics=("arbitrary", "arbitrary"),
                                             vmem_limit_bytes=VMEM_LIMIT_BYTES),
        name="prompt_mix",
    )(x, w_in_b, *tables, conv_w_rows, conv_b, conv_ln_g, conv_ln_b, ret_gn_g, w_out_b, *decay)


def _prompt_ffn_kernel(m_ref, x_ref, l1g_ref, l1b_ref, wgu_ref, wdn_ref, l2g_ref, l2b_ref,
                       *rest, tm):
    n_cast = (len(rest) - 2) // 2
    o_ref, a_scr = rest[n_cast], rest[-1]
    for src_ref, dst_ref in zip(rest[:n_cast], rest[n_cast + 1:-1]):
        dst_ref[...] = src_ref[...].astype(BF16)

    pm = tm // 2
    rows = (slice(0, pm), slice(pm, tm))
    l1g, l1b, l2g, l2b = l1g_ref[...], l1b_ref[...], l2g_ref[...], l2b_ref[...]

    def norm1(p):
        return _layer_norm(DN_ALPHA * x_ref[rows[p], :] + m_ref[rows[p], :], l1g, l1b)

    def ff_chunk(p, xb, k):
        s, w = FF_CHUNKS[k]
        gate = _dot(xb, wgu_ref[:, s:s + w])
        up = _dot(xb, wgu_ref[:, D_FF + s:D_FF + s + w])
        a_scr[rows[p], s:s + w] = (_silu(gate) * up).astype(BF16)

    def down(p):
        return _dot(a_scr[rows[p], :], wdn_ref[...])

    def norm2(p, x1, f):
        o_ref[rows[p], :] = _layer_norm(DN_ALPHA * x1 + f, l2g, l2b)

    n_ff = len(FF_CHUNKS)
    x1_0 = norm1(0)
    xb0 = x1_0.astype(BF16)
    ff_chunk(0, xb0, 0)
    x1_1 = norm1(1)
    xb1 = x1_1.astype(BF16)
    for k in range(1, n_ff):
        ff_chunk(0, xb0, k)
    f0 = down(0)
    ff_chunk(1, xb1, 0)
    norm2(0, x1_0, f0)
    for k in range(1, n_ff):
        ff_chunk(1, xb1, k)
    norm2(1, x1_1, down(1))


def _prompt_ffn(m_mix, x, ln1_g, ln1_b, w_gu_b, w_dn_b, ln2_g, ln2_b, layer, next_weights):
    m = x.shape[0]
    tm = FFN_TILE
    n_steps = m // tm
    steps_per_slab = 2
    n_slabs = n_steps // steps_per_slab
    row = lambda i: (i, 0)
    lay2 = lambda i: (layer, 0, 0)
    fixed = lambda i: (0, 0, 0)
    once = pl.Buffered(1)
    vec = pl.BlockSpec((None, 1, D_MODEL), lay2)
    cast_in, cast_out, cast_shapes = [], [], []
    for w in next_weights:
        _, rows, cols = w.shape
        slab = rows // n_slabs
        cast_in.append(pl.BlockSpec((None, slab, cols),
                                    lambda i: (layer + 1, i // steps_per_slab, 0)))
        cast_out.append(pl.BlockSpec((None, slab, cols), lambda i: (0, i // steps_per_slab, 0)))
        cast_shapes.append(jax.ShapeDtypeStruct((1, rows, cols), BF16))
    outs = pl.pallas_call(
        functools.partial(_prompt_ffn_kernel, tm=tm),
        grid=(n_steps,),
        in_specs=[pl.BlockSpec((tm, D_MODEL), row),
                  pl.BlockSpec((tm, D_MODEL), row),
                  vec, vec,
                  pl.BlockSpec((None, D_MODEL, 2 * D_FF), fixed, pipeline_mode=once),
                  pl.BlockSpec((None, D_FF, D_MODEL), fixed, pipeline_mode=once),
                  vec, vec] + cast_in,
        out_specs=[pl.BlockSpec((tm, D_MODEL), row)] + cast_out,
        out_shape=[jax.ShapeDtypeStruct((m, D_MODEL), F32)] + cast_shapes,
        scratch_shapes=[pltpu.VMEM((tm, D_FF), BF16)],
        compiler_params=pltpu.CompilerParams(dimension_semantics=("arbitrary",),
                                             vmem_limit_bytes=VMEM_LIMIT_BYTES),
        name="prompt_ffn",
    )(m_mix, x, ln1_g, ln1_b, w_gu_b, w_dn_b, ln2_g, ln2_b, *next_weights)
    return outs[0], tuple(outs[1:])


def _mix_in_kernel(x_ref, w_ref, cq_ref, sq_ref, ck_ref, sk_ref,
                   u_ref, q_ref, k_ref, v_ref, sg_ref):
    xb = x_ref[...].astype(BF16)
    ca = _dot(xb, w_ref[:, 0:CONV_CH])
    cb = _dot(xb, w_ref[:, CONV_CH:2 * CONV_CH])
    u_ref[...] = ca * jax.nn.sigmoid(cb)
    o = 2 * CONV_CH
    zq = _dot(xb, w_ref[:, o:o + RET_W])
    zk = _dot(xb, w_ref[:, o + RET_W:o + 2 * RET_W])
    cq, sq, ck, sk = cq_ref[...], sq_ref[...], ck_ref[...], sk_ref[...]
    for h in range(HEADS):
        sl = slice(h * HEAD_DIM, (h + 1) * HEAD_DIM)
        q_ref[:, sl] = _rotary(zq[:, sl], cq, sq).astype(BF16)
        k_ref[:, sl] = _rotary(zk[:, sl], ck, sk)
    v_ref[...] = _dot(xb, w_ref[:, o + 2 * RET_W:o + 3 * RET_W]).astype(BF16)
    sg_ref[...] = _silu(_dot(xb, w_ref[:, o + 3 * RET_W:o + 4 * RET_W]))


def _mix_in(x, w_in_b, layer, tables):
    m = x.shape[0]
    full = lambda w: pl.BlockSpec((m, w), lambda i: (0, 0))
    return pl.pallas_call(
        _mix_in_kernel,
        grid=(1,),
        in_specs=[full(D_MODEL),
                  pl.BlockSpec((None, D_MODEL, IN_COLS), lambda i: (0, 0, 0)),
                  full(HEAD_DIM), full(HEAD_DIM), full(HEAD_DIM), full(HEAD_DIM)],
        out_specs=[full(RET_W)] * 5,
        out_shape=[jax.ShapeDtypeStruct((m, CONV_CH), F32),
                   jax.ShapeDtypeStruct((m, RET_W), BF16),
                   jax.ShapeDtypeStruct((m, RET_W), F32),
                   jax.ShapeDtypeStruct((m, RET_W), BF16),
                   jax.ShapeDtypeStruct((m, RET_W), F32)],
        compiler_params=pltpu.CompilerParams(dimension_semantics=("arbitrary",),
                                             vmem_limit_bytes=VMEM_LIMIT_BYTES),
        name="mix_in",
    )(x, w_in_b, *tables)


def _mix_core_sample_kernel(*refs, tb, aliased):
    if aliased:
        refs = refs[:18] + refs[20:]
    (u_ref, q_ref, k_ref, v_ref, sg_ref, x_ref, cst_ref, rst_ref,
     cw_ref, cb_ref, clg_ref, clb_ref, gn_ref, wout_ref, l1g_ref, l1b_ref,
     kdec_ref, cdec_ref,
     x1_ref, cso_ref, rso_ref,
     o_scr) = refs

    u = u_ref[...]
    conv = u * cw_ref[CONV_K - 1:CONV_K, :] + cb_ref[...]
    for j in range(CONV_K - 1):
        conv = conv + cst_ref[j] * cw_ref[j:j + 1, :]
        if j > 0:
            cso_ref[j - 1] = cst_ref[j]
    cso_ref[CONV_K - 2] = u
    c = _silu(_layer_norm(conv, clg_ref[...], clb_ref[...])).astype(BF16)

    wide = tb * HEAD_DIM
    in_block = (lax.broadcasted_iota(jnp.int32, (tb, wide), 1) // HEAD_DIM
                == lax.broadcasted_iota(jnp.int32, (tb, wide), 0))
    ones_bd = jnp.where(in_block, 1.0, 0.0).astype(BF16)
    contract_rows = (((0,), (0,)), ((), ()))
    for h in range(HEADS):
        cols = slice(h * HEAD_DIM, (h + 1) * HEAD_DIM)
        kd = (k_ref[:, cols] * kdec_ref[h, 0:tb, :]).astype(BF16)
        v_bd = jnp.where(in_block, jnp.concatenate([v_ref[:, cols].astype(F32)] * tb, axis=1),
                         0.0).astype(BF16)
        kv = lax.dot_general(kd, v_bd, contract_rows, preferred_element_type=F32)
        q_cols = lax.dot_general(q_ref[:, cols], ones_bd, contract_rows,
                                 preferred_element_type=F32)
        for b in range(tb):
            blk = slice(b * HEAD_DIM, (b + 1) * HEAD_DIM)
            s_new = rst_ref[b, h] * cdec_ref[h] + kv[:, blk]
            rso_ref[b, h] = s_new
            o_scr[b:b + 1, cols] = jnp.sum(q_cols[:, blk] * s_new, axis=0, keepdims=True)

    rs = []
    for h in range(HEADS):
        cols = slice(h * HEAD_DIM, (h + 1) * HEAD_DIM)
        on = _group_norm(o_scr[:, cols], gn_ref[:, cols])
        rs.append((sg_ref[:, cols] * on).astype(BF16))
    cr = jnp.concatenate([c] + rs, axis=-1)
    m = _dot(cr, wout_ref[...])
    x1_ref[...] = _layer_norm(DN_ALPHA * x_ref[...] + m, l1g_ref[...], l1b_ref[...])


def _mix_core_sample(u, q, k, v, sg, x, state_conv, state_ret, params, layer, decay, bufs):
    conv_w, conv_b, conv_ln_g, conv_ln_b, ret_gn_g, w_out_b, ln1_g, ln1_b = params
    nb = x.shape[0]
    tb = SAMPLE_TILE
    row = lambda i: (i, 0)
    lay2 = lambda i: (layer, 0, 0)
    act = lambda w: pl.BlockSpec((tb, w), row)
    vec = lambda w: pl.BlockSpec((None, 1, w), lay2)
    dec_spec = pl.BlockSpec((HEADS, HEAD_DIM, HEAD_DIM), lambda i: (0, 0, 0))
    cst_spec = pl.BlockSpec((None, CONV_K - 1, tb, CONV_CH), lambda i: (layer, 0, i, 0))
    rst_spec = pl.BlockSpec((None, tb, HEADS, HEAD_DIM, HEAD_DIM), lambda i: (layer, i, 0, 0, 0))
    in_specs = [act(CONV_CH), act(RET_W), act(RET_W), act(RET_W), act(RET_W), act(D_MODEL),
                cst_spec, rst_spec,
                pl.BlockSpec((None, CONV_K, CONV_CH), lay2),
                vec(CONV_CH), vec(CONV_CH), vec(CONV_CH), vec(RET_W),
                pl.BlockSpec((None, D_MODEL, D_MODEL), lambda i: (0, 0, 0)),
                vec(D_MODEL), vec(D_MODEL),
                dec_spec, dec_spec]
    args = [u, q, k, v, sg, x, state_conv, state_ret,
            conv_w, conv_b, conv_ln_g, conv_ln_b, ret_gn_g, w_out_b, ln1_g, ln1_b, *decay]
    aliases = {}
    if bufs is not None:
        in_specs += [pl.BlockSpec(memory_space=pl.ANY), pl.BlockSpec(memory_space=pl.ANY)]
        args += list(bufs)
        aliases = {18: 1, 19: 2}
    return pl.pallas_call(
        functools.partial(_mix_core_sample_kernel, tb=tb, aliased=bufs is not None),
        grid=(nb // tb,),
        in_specs=in_specs,
        out_specs=[act(D_MODEL), cst_spec, rst_spec],
        out_shape=[jax.ShapeDtypeStruct((nb, D_MODEL), F32),
                   jax.ShapeDtypeStruct(state_conv.shape, F32),
                   jax.ShapeDtypeStruct(state_ret.shape, F32)],
        scratch_shapes=[pltpu.VMEM((tb, RET_W), F32)],
        input_output_aliases=aliases,
        compiler_params=pltpu.CompilerParams(dimension_semantics=("arbitrary",),
                                             vmem_limit_bytes=VMEM_LIMIT_BYTES),
        name="mix_core_sample",
    )(*args)


def _ffn_kernel(x_ref, wgu_ref, wdn_ref, g_ref, b_ref, o_ref, a_scr):
    x = x_ref[...]
    f = _swiglu_ffn(x, wgu_ref, wdn_ref, a_scr)
    o_ref[...] = _layer_norm(DN_ALPHA * x + f, g_ref[...], b_ref[...])


def _ffn(x, w_gu_b, w_dn_b, ln2_g, ln2_b, layer):
    m = x.shape[0]
    lay2 = lambda i: (layer, 0, 0)
    once = pl.Buffered(1)
    return pl.pallas_call(
        _ffn_kernel,
        grid=(1,),
        in_specs=[pl.BlockSpec((m, D_MODEL), lambda i: (0, 0)),
                  pl.BlockSpec((None, D_MODEL, 2 * D_FF), lambda i: (0, 0, 0), pipeline_mode=once),
                  pl.BlockSpec((None, D_FF, D_MODEL), lambda i: (0, 0, 0), pipeline_mode=once),
                  pl.BlockSpec((None, 1, D_MODEL), lay2),
                  pl.BlockSpec((None, 1, D_MODEL), lay2)],
        out_specs=pl.BlockSpec((m, D_MODEL), lambda i: (0, 0)),
        out_shape=jax.ShapeDtypeStruct((m, D_MODEL), F32),
        scratch_shapes=[pltpu.VMEM((m, D_FF), BF16)],
        compiler_params=pltpu.CompilerParams(dimension_semantics=("arbitrary",),
                                             vmem_limit_bytes=VMEM_LIMIT_BYTES),
        name="ffn",
    )(x, w_gu_b, w_dn_b, ln2_g, ln2_b)


def _rotary_tables(pos):
    half = HEAD_DIM // 2
    inv_freq = ROPE_BASE ** (-jnp.arange(0, half, dtype=F32) / half)
    ang = pos.astype(F32)[:, None] * inv_freq[None, :]
    cos, sin = jnp.cos(ang), jnp.sin(ang)
    cos2 = jnp.concatenate([cos, cos], axis=-1)
    sin2 = jnp.concatenate([-sin, sin], axis=-1)
    scale = HEAD_DIM ** -0.5
    return cos2, sin2, cos2 * scale, sin2 * scale


def _decay_tables(chunk):
    log_g = jnp.log(1.0 - 2.0 ** (-5.0 - jnp.arange(HEADS, dtype=F32)))
    idx = jnp.arange(chunk, dtype=F32)
    diff = idx[:, None] - idx[None, :]
    dmask = jnp.where(diff[None] >= 0,
                      jnp.exp(jnp.maximum(diff, 0.0)[None] * log_g[:, None, None]), 0.0)
    q_decay = jnp.exp((idx + 1.0)[None, :] * log_g[:, None])
    k_decay = jnp.exp((chunk - 1.0 - idx)[None, :] * log_g[:, None])
    chunk_decay = jnp.exp(chunk * log_g)
    return dmask, q_decay, k_decay, chunk_decay


def kernel(x_prompt, x_sample, state_conv, state_ret, w_in, conv_w, conv_b, conv_ln_g, conv_ln_b,
           ret_gn_g, w_out, ln1_g, ln1_b, w_gate_up, w_down, ln2_g, ln2_b):
    batch, seq, _ = x_prompt.shape
    nb = x_sample.shape[0]
    weights_f32 = (w_in, w_out, w_gate_up, w_down)
    weights_b = tuple(w[0:1].astype(BF16) for w in weights_f32)
    vec3 = lambda a: a.reshape(DEPTH, 1, a.shape[-1])
    conv_params = (conv_w, vec3(conv_b), vec3(conv_ln_g), vec3(conv_ln_b), vec3(ret_gn_g))
    conv_w_rows = jnp.broadcast_to(conv_w[:, :, None, :], (DEPTH, CONV_K, SUBLANES, CONV_CH))
    ln1_g3, ln1_b3, ln2_g3, ln2_b3 = vec3(ln1_g), vec3(ln1_b), vec3(ln2_g), vec3(ln2_b)

    tab_p = _rotary_tables(jnp.arange(seq))
    tab_s = tuple(jnp.broadcast_to(t, (nb, HEAD_DIM))
                  for t in _rotary_tables(PAST_LEN + jnp.arange(1)))

    dmask, q_dec, k_dec, c_dec = _decay_tables(CHUNK)
    bc = lambda a: jnp.broadcast_to(a, (HEADS, CHUNK, HEAD_DIM))
    bc1 = lambda a: jnp.broadcast_to(a, (HEADS, HEAD_DIM, HEAD_DIM))
    decay_p = (dmask, bc(q_dec[:, :, None]), bc(k_dec[:, :, None]), bc1(c_dec[:, None, None]))
    _, _, k_dec1, c_dec1 = _decay_tables(1)
    decay_s = (bc1(k_dec1[:, :, None]), bc1(c_dec1[:, None, None]))

    xp = x_prompt.reshape(batch * seq, D_MODEL)
    xs = x_sample.reshape(nb, D_MODEL)
    conv_p, ret_p = [], []
    bufs = None
    state_conv_t = jnp.transpose(state_conv, (0, 2, 1, 3))
    for layer in range(DEPTH):
        w_in_b, w_out_b, w_gu_b, w_dn_b = weights_b
        m_mix, cst, rst = _prompt_mix(xp, w_in_b, w_out_b, tab_p, (conv_w_rows,) + conv_params[1:],
                                      layer, decay_p, batch, seq)
        conv_p.append(cst)
        ret_p.append(rst)
        xp, weights_b = _prompt_ffn(m_mix, xp, ln1_g3, ln1_b3, w_gu_b, w_dn_b, ln2_g3, ln2_b3,
                                    layer, weights_f32 if layer + 1 < DEPTH else ())

        u, q, k, v, sg = _mix_in(xs, w_in_b, layer, tab_s)
        x1, cso, rso = _mix_core_sample(u, q, k, v, sg, xs, state_conv_t, state_ret,
                                        conv_params + (w_out_b, ln1_g3, ln1_b3), layer, decay_s, bufs)
        bufs = (cso, rso)
        xs = _ffn(x1, w_gu_b, w_dn_b, ln2_g3, ln2_b3, layer)

    return (xp.reshape(batch, seq, D_MODEL), xs.reshape(nb, 1, D_MODEL),
            jnp.stack(conv_p, axis=0), jnp.stack(ret_p, axis=0),
            jnp.transpose(bufs[0], (0, 2, 1, 3)), bufs[1])
```

```python
import functools

import jax
import jax.numpy as jnp
from jax import lax
from jax.experimental import pallas as pl
from jax.experimental.pallas import tpu as pltpu

F32 = jnp.float32
BF16 = jnp.bfloat16

D_MODEL = 1024
DEPTH = 4
CONV_CH = 512
CONV_K = 31
HEADS = 4
HEAD_DIM = 128
RET_W = HEADS * HEAD_DIM
D_FF = 2816
IN_COLS = 2 * CONV_CH + 4 * RET_W
PAST_LEN = 16384
ROPE_BASE = 10000.0
DN_ALPHA = (2 * DEPTH) ** 0.25
LN_EPS = 1e-5
CHUNK = 256

SUBLANES = 8
VMEM_LIMIT_BYTES = 56 * 1024 * 1024

HIST = 32
CONV_OFF = HIST - (CONV_K - 1)
CONV_ROWS = 64
FF_CHUNK_W = 256
FF_CHUNKS = tuple((s, min(FF_CHUNK_W, D_FF - s)) for s in range(0, D_FF, FF_CHUNK_W))
CAST_STEPS = 2

PROMPT_TILE = 512
FFN_TILE = 512
SAMPLE_TILE = 16


def _dot(a, b):
    return jnp.dot(a, b, preferred_element_type=F32)


def _dot_nt(a, b):
    return lax.dot_general(a, b, (((1,), (1,)), ((), ())), preferred_element_type=F32)


def _layer_norm(y, g, b):
    mu = jnp.mean(y, axis=-1, keepdims=True)
    d = y - mu
    var = jnp.mean(d * d, axis=-1, keepdims=True)
    return d * lax.rsqrt(var + LN_EPS) * g + b


def _group_norm(o, g):
    mu = jnp.mean(o, axis=-1, keepdims=True)
    d = o - mu
    var = jnp.mean(d * d, axis=-1, keepdims=True)
    return d * lax.rsqrt(var + LN_EPS) * g


def _silu(x):
    return x * jax.nn.sigmoid(x)


def _rotary(seg, cos, sin_signed):
    return seg * cos + pltpu.roll(seg, HEAD_DIM // 2, 1) * sin_signed


def _shift_copies(win_ref, sh_ref, lo, hi):
    for r in range(1, SUBLANES):
        sh_ref[r - 1, lo:hi, :] = win_ref[r + lo:r + hi, :]


def _conv_block(win_ref, sh_ref, cw_ref, cbias, clg, clb, r0):
    acc = jnp.broadcast_to(cbias, (CONV_ROWS, CONV_CH))
    for j in range(CONV_K):
        a, r = divmod(CONV_OFF + j, SUBLANES)
        lo = r0 + a * SUBLANES
        src = win_ref[lo:lo + CONV_ROWS, :] if r == 0 else sh_ref[r - 1, lo:lo + CONV_ROWS, :]
        acc = acc + src * jnp.concatenate([cw_ref[j]] * (CONV_ROWS // SUBLANES), axis=0)
    return _silu(_layer_norm(acc, clg, clb)).astype(BF16)


def _swiglu_ffn(x1, wgu_ref, wdn_ref, a_scr):
    xb = x1.astype(BF16)
    for s, w in FF_CHUNKS:
        gate = _dot(xb, wgu_ref[:, s:s + w])
        up = _dot(xb, wgu_ref[:, D_FF + s:D_FF + s + w])
        a_scr[:, s:s + w] = (_silu(gate) * up).astype(BF16)
    return _dot(a_scr[...], wdn_ref[...])


def _cast_specs(weights, layer, n_steps, step):
    n_slabs = n_steps // CAST_STEPS
    ins, outs, shapes = [], [], []
    for w in weights:
        _, rows, cols = w.shape
        slab = rows // n_slabs
        ins.append(pl.BlockSpec((None, slab, cols), lambda *g: (layer, step(*g) // CAST_STEPS, 0)))
        outs.append(pl.BlockSpec((None, slab, cols), lambda *g: (0, step(*g) // CAST_STEPS, 0)))
        shapes.append(jax.ShapeDtypeStruct((1, rows, cols), BF16))
    return ins, outs, shapes


def _cast_slabs(src_refs, dst_refs):
    for src_ref, dst_ref in zip(src_refs, dst_refs):
        dst_ref[...] = src_ref[...].astype(BF16)


def _prompt_mix_kernel(x_ref, w_ref, cq_ref, sq_ref, ck_ref, sk_ref,
                       cw_ref, cb_ref, clg_ref, clb_ref, gn_ref, wout_ref,
                       dmask_ref, qdec_ref, kdec_ref, cdec_ref, *rest, tm):
    n_cast = (len(rest) - 12) // 2
    m_ref, convst_ref, retst_ref = rest[n_cast:n_cast + 3]
    ubuf, shbuf, cr_ref, s_scr, q_scr, k_scr, kd_scr, v_scr, sg_scr = rest[2 * n_cast + 3:]
    l = pl.program_id(1)

    @pl.when(l == 0)
    def _():
        ubuf[0:HIST, :] = jnp.zeros((HIST, CONV_CH), F32)
        s_scr[...] = jnp.zeros_like(s_scr)

    _cast_slabs(rest[:n_cast], rest[n_cast + 3:2 * n_cast + 3])
    xb = x_ref[...].astype(BF16)
    n_chunks = tm // CHUNK

    half = tm // 2
    halo = HIST - SUBLANES

    def glu(p):
        rows = slice(p * half, (p + 1) * half)
        ca = _dot(xb[rows, :], w_ref[:, 0:CONV_CH])
        cb = _dot(xb[rows, :], w_ref[:, CONV_CH:2 * CONV_CH])
        ubuf[HIST + p * half:HIST + (p + 1) * half, :] = ca * jax.nn.sigmoid(cb)
        _shift_copies(ubuf, shbuf, p * half + (halo if p else 0), (p + 1) * half + halo)

    clg, clb, cbias = clg_ref[...], clb_ref[...], cb_ref[...]

    def conv_block(rb):
        r0 = rb * CONV_ROWS
        cr_ref[r0:r0 + CONV_ROWS, 0:CONV_CH] = _conv_block(ubuf, shbuf, cw_ref, cbias, clg, clb, r0)

    o = 2 * CONV_CH
    cq, sq, ck, sk = cq_ref[...], sq_ref[...], ck_ref[...], sk_ref[...]

    def proj_q():
        zq = _dot(xb, w_ref[:, o:o + RET_W])
        for h in range(HEADS):
            cols = slice(h * HEAD_DIM, (h + 1) * HEAD_DIM)
            q_scr[:, cols] = _rotary(zq[:, cols], cq, sq).astype(BF16)

    def proj_k():
        zk = _dot(xb, w_ref[:, o + RET_W:o + 2 * RET_W])
        for h in range(HEADS):
            cols = slice(h * HEAD_DIM, (h + 1) * HEAD_DIM)
            kr = _rotary(zk[:, cols], ck, sk)
            k_scr[:, cols] = kr.astype(BF16)
            for c in range(n_chunks):
                rows = slice(c * CHUNK, (c + 1) * CHUNK)
                kd_scr[rows, cols] = (kr[rows, :] * kdec_ref[h]).astype(BF16)

    def proj_v():
        v_scr[...] = _dot(xb, w_ref[:, o + 2 * RET_W:o + 3 * RET_W]).astype(BF16)

    def proj_gate():
        sg_scr[...] = _silu(_dot(xb, w_ref[:, o + 3 * RET_W:o + 4 * RET_W]))

    def retention(c, h):
        rows = slice(c * CHUNK, (c + 1) * CHUNK)
        cols = slice(h * HEAD_DIM, (h + 1) * HEAD_DIM)
        qc = q_scr[rows, cols]
        vc = v_scr[rows, cols]
        s = s_scr[h]
        sc = _dot_nt(qc, k_scr[rows, cols]) * dmask_ref[h]
        out = _dot(sc.astype(BF16), vc) + _dot(qc, s.astype(BF16)) * qdec_ref[h]
        s_scr[h] = s * cdec_ref[h] + _dot(kd_scr[rows, cols].T, vc)
        on = _group_norm(out, gn_ref[:, cols])
        cr_ref[rows, CONV_CH + h * HEAD_DIM:CONV_CH + (h + 1) * HEAD_DIM] = (
            sg_scr[rows, cols] * on).astype(BF16)

    def out_proj(c):
        rows = slice(c * CHUNK, (c + 1) * CHUNK)
        m_ref[rows, :] = _dot(cr_ref[rows, :], wout_ref[...])

    glu(0)
    glu(1)
    stages = [proj_q, proj_k, proj_v, proj_gate]
    proj_stage = []
    for c in range(n_chunks):
        stages += [functools.partial(retention, c, h) for h in range(HEADS)]
        proj_stage.append(len(stages))
        stages.append(functools.partial(out_proj, c))
    n_blocks = tm // CONV_ROWS
    blocks_per_chunk = CHUNK // CONV_ROWS
    assert all((c + 1) * blocks_per_chunk <= proj_stage[c] for c in range(n_chunks))
    for p, stage in enumerate(stages):
        stage()
        if p < n_blocks:
            conv_block(p)

    ubuf[0:HIST, :] = ubuf[tm:tm + HIST, :]

    @pl.when(l == pl.num_programs(1) - 1)
    def _():
        convst_ref[...] = ubuf[CONV_OFF:HIST, :]
        retst_ref[...] = s_scr[...]


def _prompt_mix(x, w_in_b, w_out_b, tables, params, layer, decay, batch, seq, cast_weights):
    conv_w_rows, conv_b, conv_ln_g, conv_ln_b, ret_gn_g = params
    tm = PROMPT_TILE
    nl = seq // tm
    row = lambda b, l: (b * nl + l, 0)
    lay2 = lambda b, l: (layer, 0, 0)
    vec = lambda w: pl.BlockSpec((None, 1, w), lay2)
    tab_spec = pl.BlockSpec((tm, HEAD_DIM), lambda b, l: (l, 0))
    dec_spec = lambda rows, cols: pl.BlockSpec((HEADS, rows, cols), lambda b, l: (0, 0, 0))
    cast_in, cast_out, cast_shapes = _cast_specs(cast_weights, layer, batch * nl,
                                                 lambda b, l: b * nl + l)
    outs = pl.pallas_call(
        functools.partial(_prompt_mix_kernel, tm=tm),
        grid=(batch, nl),
        in_specs=[pl.BlockSpec((tm, D_MODEL), row),
                  pl.BlockSpec((None, D_MODEL, IN_COLS), lambda b, l: (0, 0, 0),
                               pipeline_mode=pl.Buffered(1)),
                  tab_spec, tab_spec, tab_spec, tab_spec,
                  pl.BlockSpec((None, CONV_K, SUBLANES, CONV_CH), lambda b, l: (layer, 0, 0, 0)),
                  vec(CONV_CH), vec(CONV_CH), vec(CONV_CH), vec(RET_W),
                  pl.BlockSpec((None, D_MODEL, D_MODEL), lambda b, l: (0, 0, 0),
                               pipeline_mode=pl.Buffered(1)),
                  dec_spec(CHUNK, CHUNK), dec_spec(CHUNK, HEAD_DIM), dec_spec(CHUNK, HEAD_DIM),
                  dec_spec(HEAD_DIM, HEAD_DIM)] + cast_in,
        out_specs=[pl.BlockSpec((tm, D_MODEL), row),
                   pl.BlockSpec((None, CONV_K - 1, CONV_CH), lambda b, l: (b, 0, 0)),
                   pl.BlockSpec((None, HEADS, HEAD_DIM, HEAD_DIM), lambda b, l: (b, 0, 0, 0))]
        + cast_out,
        out_shape=[jax.ShapeDtypeStruct((batch * seq, D_MODEL), F32),
                   jax.ShapeDtypeStruct((batch, CONV_K - 1, CONV_CH), F32),
                   jax.ShapeDtypeStruct((batch, HEADS, HEAD_DIM, HEAD_DIM), F32)] + cast_shapes,
        scratch_shapes=[pltpu.VMEM((HIST + tm, CONV_CH), F32),
                        pltpu.VMEM((SUBLANES - 1, HIST + tm - SUBLANES, CONV_CH), F32),
                        pltpu.VMEM((tm, D_MODEL), BF16),
                        pltpu.VMEM((HEADS, HEAD_DIM, HEAD_DIM), F32),
                        pltpu.VMEM((tm, RET_W), BF16),
                        pltpu.VMEM((tm, RET_W), BF16),
                        pltpu.VMEM((tm, RET_W), BF16),
                        pltpu.VMEM((tm, RET_W), BF16),
                        pltpu.VMEM((tm, RET_W), F32)],
        compiler_params=pltpu.CompilerParams(dimension_semantics=("arbitrary", "arbitrary"),
                                             vmem_limit_bytes=VMEM_LIMIT_BYTES),
        name="prompt_mix",
    )(x, w_in_b, *tables, conv_w_rows, conv_b, conv_ln_g, conv_ln_b, ret_gn_g, w_out_b, *decay,
      *cast_weights)
    return outs[0], outs[1], outs[2], tuple(outs[3:])


def _prompt_ffn_kernel(m_ref, x_ref, l1g_ref, l1b_ref, wgu_ref, wdn_ref, l2g_ref, l2b_ref,
                       *rest, tm):
    n_cast = (len(rest) - 2) // 2
    o_ref, a_scr = rest[n_cast], rest[-1]
    _cast_slabs(rest[:n_cast], rest[n_cast + 1:-1])

    pm = tm // 2
    rows = (slice(0, pm), slice(pm, tm))
    l1g, l1b, l2g, l2b = l1g_ref[...], l1b_ref[...], l2g_ref[...], l2b_ref[...]

    def norm1(p):
        return _layer_norm(DN_ALPHA * x_ref[rows[p], :] + m_ref[rows[p], :], l1g, l1b)

    def ff_chunk(p, xb, k):
        s, w = FF_CHUNKS[k]
        gate = _dot(xb, wgu_ref[:, s:s + w])
        up = _dot(xb, wgu_ref[:, D_FF + s:D_FF + s + w])
        a_scr[rows[p], s:s + w] = (_silu(gate) * up).astype(BF16)

    def down(p):
        return _dot(a_scr[rows[p], :], wdn_ref[...])

    def norm2(p, x1, f):
        o_ref[rows[p], :] = _layer_norm(DN_ALPHA * x1 + f, l2g, l2b)

    n_ff = len(FF_CHUNKS)
    x1_0 = norm1(0)
    xb0 = x1_0.astype(BF16)
    ff_chunk(0, xb0, 0)
    x1_1 = norm1(1)
    xb1 = x1_1.astype(BF16)
    for k in range(1, n_ff):
        ff_chunk(0, xb0, k)
    f0 = down(0)
    ff_chunk(1, xb1, 0)
    norm2(0, x1_0, f0)
    for k in range(1, n_ff):
        ff_chunk(1, xb1, k)
    norm2(1, x1_1, down(1))


def _prompt_ffn(m_mix, x, ln1_g, ln1_b, w_gu_b, w_dn_b, ln2_g, ln2_b, layer, next_weights):
    m = x.shape[0]
    tm = FFN_TILE
    n_steps = m // tm
    row = lambda i: (i, 0)
    lay2 = lambda i: (layer, 0, 0)
    fixed = lambda i: (0, 0, 0)
    once = pl.Buffered(1)
    vec = pl.BlockSpec((None, 1, D_MODEL), lay2)
    cast_in, cast_out, cast_shapes = _cast_specs(next_weights, layer + 1, n_steps, lambda i: i)
    outs = pl.pallas_call(
        functools.partial(_prompt_ffn_kernel, tm=tm),
        grid=(n_steps,),
        in_specs=[pl.BlockSpec((tm, D_MODEL), row),
                  pl.BlockSpec((tm, D_MODEL), row),
                  vec, vec,
                  pl.BlockSpec((None, D_MODEL, 2 * D_FF), fixed, pipeline_mode=once),
                  pl.BlockSpec((None, D_FF, D_MODEL), fixed, pipeline_mode=once),
                  vec, vec] + cast_in,
        out_specs=[pl.BlockSpec((tm, D_MODEL), row)] + cast_out,
        out_shape=[jax.ShapeDtypeStruct((m, D_MODEL), F32)] + cast_shapes,
        scratch_shapes=[pltpu.VMEM((tm, D_FF), BF16)],
        compiler_params=pltpu.CompilerParams(dimension_semantics=("arbitrary",),
                                             vmem_limit_bytes=VMEM_LIMIT_BYTES),
        name="prompt_ffn",
    )(m_mix, x, ln1_g, ln1_b, w_gu_b, w_dn_b, ln2_g, ln2_b, *next_weights)
    return outs[0], tuple(outs[1:])


def _mix_in_kernel(x_ref, w_ref, cq_ref, sq_ref, ck_ref, sk_ref,
                   u_ref, q_ref, k_ref, v_ref, sg_ref):
    xb = x_ref[...].astype(BF16)
    ca = _dot(xb, w_ref[:, 0:CONV_CH])
    cb = _dot(xb, w_ref[:, CONV_CH:2 * CONV_CH])
    u_ref[...] = ca * jax.nn.sigmoid(cb)
    o = 2 * CONV_CH
    zq = _dot(xb, w_ref[:, o:o + RET_W])
    zk = _dot(xb, w_ref[:, o + RET_W:o + 2 * RET_W])
    cq, sq, ck, sk = cq_ref[...], sq_ref[...], ck_ref[...], sk_ref[...]
    for h in range(HEADS):
        sl = slice(h * HEAD_DIM, (h + 1) * HEAD_DIM)
        q_ref[:, sl] = _rotary(zq[:, sl], cq, sq).astype(BF16)
        k_ref[:, sl] = _rotary(zk[:, sl], ck, sk)
    v_ref[...] = _dot(xb, w_ref[:, o + 2 * RET_W:o + 3 * RET_W]).astype(BF16)
    sg_ref[...] = _silu(_dot(xb, w_ref[:, o + 3 * RET_W:o + 4 * RET_W]))


def _mix_in(x, w_in_b, layer, tables):
    m = x.shape[0]
    full = lambda w: pl.BlockSpec((m, w), lambda i: (0, 0))
    return pl.pallas_call(
        _mix_in_kernel,
        grid=(1,),
        in_specs=[full(D_MODEL),
                  pl.BlockSpec((None, D_MODEL, IN_COLS), lambda i: (0, 0, 0)),
                  full(HEAD_DIM), full(HEAD_DIM), full(HEAD_DIM), full(HEAD_DIM)],
        out_specs=[full(RET_W)] * 5,
        out_shape=[jax.ShapeDtypeStruct((m, CONV_CH), F32),
                   jax.ShapeDtypeStruct((m, RET_W), BF16),
                   jax.ShapeDtypeStruct((m, RET_W), F32),
                   jax.ShapeDtypeStruct((m, RET_W), BF16),
                   jax.ShapeDtypeStruct((m, RET_W), F32)],
        compiler_params=pltpu.CompilerParams(dimension_semantics=("arbitrary",),
                                             vmem_limit_bytes=VMEM_LIMIT_BYTES),
        name="mix_in",
    )(x, w_in_b, *tables)


def _mix_core_sample_kernel(*refs, tb, aliased):
    if aliased:
        refs = refs[:18] + refs[20:]
    (u_ref, q_ref, k_ref, v_ref, sg_ref, x_ref, cst_ref, rst_ref,
     cw_ref, cb_ref, clg_ref, clb_ref, gn_ref, wout_ref, l1g_ref, l1b_ref,
     kdec_ref, cdec_ref,
     x1_ref, cso_ref, rso_ref,
     o_scr) = refs

    u = u_ref[...]
    conv = u * cw_ref[CONV_K - 1:CONV_K, :] + cb_ref[...]
    for j in range(CONV_K - 1):
        conv = conv + cst_ref[j] * cw_ref[j:j + 1, :]
        if j > 0:
            cso_ref[j - 1] = cst_ref[j]
    cso_ref[CONV_K - 2] = u
    c = _silu(_layer_norm(conv, clg_ref[...], clb_ref[...])).astype(BF16)

    wide = tb * HEAD_DIM
    in_block = (lax.broadcasted_iota(jnp.int32, (tb, wide), 1) // HEAD_DIM
                == lax.broadcasted_iota(jnp.int32, (tb, wide), 0))
    ones_bd = jnp.where(in_block, 1.0, 0.0).astype(BF16)
    contract_rows = (((0,), (0,)), ((), ()))
    for h in range(HEADS):
        cols = slice(h * HEAD_DIM, (h + 1) * HEAD_DIM)
        kd = (k_ref[:, cols] * kdec_ref[h, 0:tb, :]).astype(BF16)
        v_bd = jnp.where(in_block, jnp.concatenate([v_ref[:, cols].astype(F32)] * tb, axis=1),
                         0.0).astype(BF16)
        kv = lax.dot_general(kd, v_bd, contract_rows, preferred_element_type=F32)
        q_cols = lax.dot_general(q_ref[:, cols], ones_bd, contract_rows,
                                 preferred_element_type=F32)
        for b in range(tb):
            blk = slice(b * HEAD_DIM, (b + 1) * HEAD_DIM)
            s_new = rst_ref[b, h] * cdec_ref[h] + kv[:, blk]
            rso_ref[b, h] = s_new
            o_scr[b:b + 1, cols] = jnp.sum(q_cols[:, blk] * s_new, axis=0, keepdims=True)

    rs = []
    for h in range(HEADS):
        cols = slice(h * HEAD_DIM, (h + 1) * HEAD_DIM)
        on = _group_norm(o_scr[:, cols], gn_ref[:, cols])
        rs.append((sg_ref[:, cols] * on).astype(BF16))
    cr = jnp.concatenate([c] + rs, axis=-1)
    m = _dot(cr, wout_ref[...])
    x1_ref[...] = _layer_norm(DN_ALPHA * x_ref[...] + m, l1g_ref[...], l1b_ref[...])


def _mix_core_sample(u, q, k, v, sg, x, state_conv, state_ret, params, layer, decay, bufs):
    conv_w, conv_b, conv_ln_g, conv_ln_b, ret_gn_g, w_out_b, ln1_g, ln1_b = params
    nb = x.shape[0]
    tb = SAMPLE_TILE
    row = lambda i: (i, 0)
    lay2 = lambda i: (layer, 0, 0)
    act = lambda w: pl.BlockSpec((tb, w), row)
    vec = lambda w: pl.BlockSpec((None, 1, w), lay2)
    dec_spec = pl.BlockSpec((HEADS, HEAD_DIM, HEAD_DIM), lambda i: (0, 0, 0))
    cst_spec = pl.BlockSpec((None, CONV_K - 1, tb, CONV_CH), lambda i: (layer, 0, i, 0))
    rst_spec = pl.BlockSpec((None, tb, HEADS, HEAD_DIM, HEAD_DIM), lambda i: (layer, i, 0, 0, 0))
    in_specs = [act(CONV_CH), act(RET_W), act(RET_W), act(RET_W), act(RET_W), act(D_MODEL),
                cst_spec, rst_spec,
                pl.BlockSpec((None, CONV_K, CONV_CH), lay2),
                vec(CONV_CH), vec(CONV_CH), vec(CONV_CH), vec(RET_W),
                pl.BlockSpec((None, D_MODEL, D_MODEL), lambda i: (0, 0, 0)),
                vec(D_MODEL), vec(D_MODEL),
                dec_spec, dec_spec]
    args = [u, q, k, v, sg, x, state_conv, state_ret,
            conv_w, conv_b, conv_ln_g, conv_ln_b, ret_gn_g, w_out_b, ln1_g, ln1_b, *decay]
    aliases = {}
    if bufs is not None:
        in_specs += [pl.BlockSpec(memory_space=pl.ANY), pl.BlockSpec(memory_space=pl.ANY)]
        args += list(bufs)
        aliases = {18: 1, 19: 2}
    return pl.pallas_call(
        functools.partial(_mix_core_sample_kernel, tb=tb, aliased=bufs is not None),
        grid=(nb // tb,),
        in_specs=in_specs,
        out_specs=[act(D_MODEL), cst_spec, rst_spec],
        out_shape=[jax.ShapeDtypeStruct((nb, D_MODEL), F32),
                   jax.ShapeDtypeStruct(state_conv.shape, F32),
                   jax.ShapeDtypeStruct(state_ret.shape, F32)],
        scratch_shapes=[pltpu.VMEM((tb, RET_W), F32)],
        input_output_aliases=aliases,
        compiler_params=pltpu.CompilerParams(dimension_semantics=("arbitrary",),
                                             vmem_limit_bytes=VMEM_LIMIT_BYTES),
        name="mix_core_sample",
    )(*args)


def _ffn_kernel(x_ref, wgu_ref, wdn_ref, g_ref, b_ref, o_ref, a_scr):
    x = x_ref[...]
    f = _swiglu_ffn(x, wgu_ref, wdn_ref, a_scr)
    o_ref[...] = _layer_norm(DN_ALPHA * x + f, g_ref[...], b_ref[...])


def _ffn(x, w_gu_b, w_dn_b, ln2_g, ln2_b, layer):
    m = x.shape[0]
    lay2 = lambda i: (layer, 0, 0)
    once = pl.Buffered(1)
    return pl.pallas_call(
        _ffn_kernel,
        grid=(1,),
        in_specs=[pl.BlockSpec((m, D_MODEL), lambda i: (0, 0)),
                  pl.BlockSpec((None, D_MODEL, 2 * D_FF), lambda i: (0, 0, 0), pipeline_mode=once),
                  pl.BlockSpec((None, D_FF, D_MODEL), lambda i: (0, 0, 0), pipeline_mode=once),
                  pl.BlockSpec((None, 1, D_MODEL), lay2),
                  pl.BlockSpec((None, 1, D_MODEL), lay2)],
        out_specs=pl.BlockSpec((m, D_MODEL), lambda i: (0, 0)),
        out_shape=jax.ShapeDtypeStruct((m, D_MODEL), F32),
        scratch_shapes=[pltpu.VMEM((m, D_FF), BF16)],
        compiler_params=pltpu.CompilerParams(dimension_semantics=("arbitrary",),
                                             vmem_limit_bytes=VMEM_LIMIT_BYTES),
        name="ffn",
    )(x, w_gu_b, w_dn_b, ln2_g, ln2_b)


def _rotary_tables(pos):
    half = HEAD_DIM // 2
    inv_freq = ROPE_BASE ** (-jnp.arange(0, half, dtype=F32) / half)
    ang = pos.astype(F32)[:, None] * inv_freq[None, :]
    cos, sin = jnp.cos(ang), jnp.sin(ang)
    cos2 = jnp.concatenate([cos, cos], axis=-1)
    sin2 = jnp.concatenate([-sin, sin], axis=-1)
    scale = HEAD_DIM ** -0.5
    return cos2, sin2, cos2 * scale, sin2 * scale


def _decay_tables(chunk):
    log_g = jnp.log(1.0 - 2.0 ** (-5.0 - jnp.arange(HEADS, dtype=F32)))
    idx = jnp.arange(chunk, dtype=F32)
    diff = idx[:, None] - idx[None, :]
    dmask = jnp.where(diff[None] >= 0,
                      jnp.exp(jnp.maximum(diff, 0.0)[None] * log_g[:, None, None]), 0.0)
    q_decay = jnp.exp((idx + 1.0)[None, :] * log_g[:, None])
    k_decay = jnp.exp((chunk - 1.0 - idx)[None, :] * log_g[:, None])
    chunk_decay = jnp.exp(chunk * log_g)
    return dmask, q_decay, k_decay, chunk_decay


def kernel(x_prompt, x_sample, state_conv, state_ret, w_in, conv_w, conv_b, conv_ln_g, conv_ln_b,
           ret_gn_g, w_out, ln1_g, ln1_b, w_gate_up, w_down, ln2_g, ln2_b):
    batch, seq, _ = x_prompt.shape
    nb = x_sample.shape[0]
    weights_f32 = (w_in, w_out, w_gate_up, w_down)
    weights_b = (w_in[0:1].astype(BF16), w_out[0:1].astype(BF16))
    vec3 = lambda a: a.reshape(DEPTH, 1, a.shape[-1])
    conv_params = (conv_w, vec3(conv_b), vec3(conv_ln_g), vec3(conv_ln_b), vec3(ret_gn_g))
    conv_w_rows = jnp.broadcast_to(conv_w[:, :, None, :], (DEPTH, CONV_K, SUBLANES, CONV_CH))
    ln1_g3, ln1_b3, ln2_g3, ln2_b3 = vec3(ln1_g), vec3(ln1_b), vec3(ln2_g), vec3(ln2_b)

    tab_p = _rotary_tables(jnp.arange(seq))
    tab_s = tuple(jnp.broadcast_to(t, (nb, HEAD_DIM))
                  for t in _rotary_tables(PAST_LEN + jnp.arange(1)))

    dmask, q_dec, k_dec, c_dec = _decay_tables(CHUNK)
    bc = lambda a: jnp.broadcast_to(a, (HEADS, CHUNK, HEAD_DIM))
    bc1 = lambda a: jnp.broadcast_to(a, (HEADS, HEAD_DIM, HEAD_DIM))
    decay_p = (dmask, bc(q_dec[:, :, None]), bc(k_dec[:, :, None]), bc1(c_dec[:, None, None]))
    _, _, k_dec1, c_dec1 = _decay_tables(1)
    decay_s = (bc1(k_dec1[:, :, None]), bc1(c_dec1[:, None, None]))

    xp = x_prompt.reshape(batch * seq, D_MODEL)
    xs = x_sample.reshape(nb, D_MODEL)
    conv_p, ret_p = [], []
    bufs = None
    state_conv_t = jnp.transpose(state_conv, (0, 2, 1, 3))
    for layer in range(DEPTH):
        m_mix, cst, rst, ffn_b = _prompt_mix(
            xp, weights_b[0], weights_b[1], tab_p, (conv_w_rows,) + conv_params[1:], layer, decay_p,
            batch, seq, weights_f32[2:] if layer == 0 else ())
        w_in_b, w_out_b, w_gu_b, w_dn_b = weights_b + ffn_b
        conv_p.append(cst)
        ret_p.append(rst)
        xp, weights_b = _prompt_ffn(m_mix, xp, ln1_g3, ln1_b3, w_gu_b, w_dn_b, ln2_g3, ln2_b3,
                                    layer, weights_f32 if layer + 1 < DEPTH else ())

        u, q, k, v, sg = _mix_in(xs, w_in_b, layer, tab_s)
        x1, cso, rso = _mix_core_sample(u, q, k, v, sg, xs, state_conv_t, state_ret,
                                        conv_params + (w_out_b, ln1_g3, ln1_b3), layer, decay_s, bufs)
        bufs = (cso, rso)
        xs = _ffn(x1, w_gu_b, w_dn_b, ln2_g3, ln2_b3, layer)

    return (xp.reshape(batch, seq, D_MODEL), xs.reshape(nb, 1, D_MODEL),
            jnp.stack(conv_p, axis=0), jnp.stack(ret_p, axis=0),
            jnp.transpose(bufs[0], (0, 2, 1, 3)), bufs[1])
```

```python
import functools

import jax
import jax.numpy as jnp
import numpy as np
from jax import lax
from jax.experimental import pallas as pl
from jax.experimental.pallas import tpu as pltpu

F32 = jnp.float32
BF16 = jnp.bfloat16

D_MODEL = 1024
DEPTH = 4
CONV_CH = 512
CONV_K = 31
HEADS = 4
HEAD_DIM = 128
RET_W = HEADS * HEAD_DIM
D_FF = 2816
IN_COLS = 2 * CONV_CH + 4 * RET_W
PAST_LEN = 16384
ROPE_BASE = 10000.0
DN_ALPHA = (2 * DEPTH) ** 0.25
LN_EPS = 1e-5
CHUNK = 256

SUBLANES = 8
VMEM_LIMIT_BYTES = 56 * 1024 * 1024

HIST = 32
CONV_OFF = HIST - (CONV_K - 1)
CONV_ROWS = 64
FF_CHUNK_W = 256
FF_CHUNKS = tuple((s, min(FF_CHUNK_W, D_FF - s)) for s in range(0, D_FF, FF_CHUNK_W))
CAST_STEPS = 2

PROMPT_TILE = 512
FFN_TILE = 512
SAMPLE_TILE = 16


def _dot(a, b):
    return jnp.dot(a, b, preferred_element_type=F32)


def _dot_nt(a, b):
    return lax.dot_general(a, b, (((1,), (1,)), ((), ())), preferred_element_type=F32)


def _layer_norm(y, g, b):
    mu = jnp.mean(y, axis=-1, keepdims=True)
    d = y - mu
    var = jnp.mean(d * d, axis=-1, keepdims=True)
    return d * lax.rsqrt(var + LN_EPS) * g + b


def _group_norm(o, g):
    mu = jnp.mean(o, axis=-1, keepdims=True)
    d = o - mu
    var = jnp.mean(d * d, axis=-1, keepdims=True)
    return d * lax.rsqrt(var + LN_EPS) * g


def _silu(x):
    return x * jax.nn.sigmoid(x)


def _row(ref, layer):
    return ref[layer:layer + 1, :]


def _rotary(seg, cos, sin_signed):
    return seg * cos + pltpu.roll(seg, HEAD_DIM // 2, 1) * sin_signed


def _shift_copies(win_ref, sh_ref, lo, hi):
    for r in range(1, SUBLANES):
        sh_ref[r - 1, lo:hi, :] = win_ref[r + lo:r + hi, :]


def _conv_block(win_ref, sh_ref, cw_ref, cbias, clg, clb, r0):
    acc = jnp.broadcast_to(cbias, (CONV_ROWS, CONV_CH))
    for j in range(CONV_K):
        a, r = divmod(CONV_OFF + j, SUBLANES)
        lo = r0 + a * SUBLANES
        src = win_ref[lo:lo + CONV_ROWS, :] if r == 0 else sh_ref[r - 1, lo:lo + CONV_ROWS, :]
        acc = acc + src * jnp.concatenate([cw_ref[j]] * (CONV_ROWS // SUBLANES), axis=0)
    return _silu(_layer_norm(acc, clg, clb)).astype(BF16)


def _swiglu_ffn(x1, wgu_ref, wdn_ref, a_scr):
    xb = x1.astype(BF16)
    for s, w in FF_CHUNKS:
        gate = _dot(xb, wgu_ref[:, s:s + w])
        up = _dot(xb, wgu_ref[:, D_FF + s:D_FF + s + w])
        a_scr[:, s:s + w] = (_silu(gate) * up).astype(BF16)
    return _dot(a_scr[...], wdn_ref[...])


def _cast_specs(weights, layer, n_steps, step):
    n_slabs = n_steps // CAST_STEPS
    ins, outs, shapes = [], [], []
    for w in weights:
        _, rows, cols = w.shape
        slab = rows // n_slabs
        ins.append(pl.BlockSpec((None, slab, cols), lambda *g: (layer, step(*g) // CAST_STEPS, 0)))
        outs.append(pl.BlockSpec((None, slab, cols), lambda *g: (0, step(*g) // CAST_STEPS, 0)))
        shapes.append(jax.ShapeDtypeStruct((1, rows, cols), BF16))
    return ins, outs, shapes


def _cast_slabs(src_refs, dst_refs):
    for src_ref, dst_ref in zip(src_refs, dst_refs):
        dst_ref[...] = src_ref[...].astype(BF16)


def _prompt_mix_kernel(x_ref, w_ref, cq_ref, sq_ref, ck_ref, sk_ref,
                       cw_ref, cb_ref, clg_ref, clb_ref, gn_ref, wout_ref,
                       dmask_ref, qdec_ref, kdec_ref, cdec_ref, *rest, tm, layer):
    n_cast = (len(rest) - 12) // 2
    m_ref, convst_ref, retst_ref = rest[n_cast:n_cast + 3]
    ubuf, shbuf, cr_ref, s_scr, q_scr, k_scr, kd_scr, v_scr, sg_scr = rest[2 * n_cast + 3:]
    l = pl.program_id(1)

    @pl.when(l == 0)
    def _():
        ubuf[0:HIST, :] = jnp.zeros((HIST, CONV_CH), F32)
        s_scr[...] = jnp.zeros_like(s_scr)

    _cast_slabs(rest[:n_cast], rest[n_cast + 3:2 * n_cast + 3])
    xb = x_ref[...].astype(BF16)
    n_chunks = tm // CHUNK

    half = tm // 2
    halo = HIST - SUBLANES

    def glu(p):
        rows = slice(p * half, (p + 1) * half)
        ca = _dot(xb[rows, :], w_ref[:, 0:CONV_CH])
        cb = _dot(xb[rows, :], w_ref[:, CONV_CH:2 * CONV_CH])
        ubuf[HIST + p * half:HIST + (p + 1) * half, :] = ca * jax.nn.sigmoid(cb)
        _shift_copies(ubuf, shbuf, p * half + (halo if p else 0), (p + 1) * half + halo)

    clg, clb, cbias = _row(clg_ref, layer), _row(clb_ref, layer), _row(cb_ref, layer)

    def conv_block(rb):
        r0 = rb * CONV_ROWS
        cr_ref[r0:r0 + CONV_ROWS, 0:CONV_CH] = _conv_block(ubuf, shbuf, cw_ref, cbias, clg, clb, r0)

    o = 2 * CONV_CH
    cq, sq, ck, sk = cq_ref[...], sq_ref[...], ck_ref[...], sk_ref[...]

    def proj_q():
        zq = _dot(xb, w_ref[:, o:o + RET_W])
        for h in range(HEADS):
            cols = slice(h * HEAD_DIM, (h + 1) * HEAD_DIM)
            q_scr[:, cols] = _rotary(zq[:, cols], cq, sq).astype(BF16)

    def proj_k():
        zk = _dot(xb, w_ref[:, o + RET_W:o + 2 * RET_W])
        for h in range(HEADS):
            cols = slice(h * HEAD_DIM, (h + 1) * HEAD_DIM)
            kr = _rotary(zk[:, cols], ck, sk)
            k_scr[:, cols] = kr.astype(BF16)
            for c in range(n_chunks):
                rows = slice(c * CHUNK, (c + 1) * CHUNK)
                kd_scr[rows, cols] = (kr[rows, :] * kdec_ref[h]).astype(BF16)

    def proj_v():
        v_scr[...] = _dot(xb, w_ref[:, o + 2 * RET_W:o + 3 * RET_W]).astype(BF16)

    def proj_gate():
        sg_scr[...] = _silu(_dot(xb, w_ref[:, o + 3 * RET_W:o + 4 * RET_W]))

    def retention(c, h):
        rows = slice(c * CHUNK, (c + 1) * CHUNK)
        cols = slice(h * HEAD_DIM, (h + 1) * HEAD_DIM)
        qc = q_scr[rows, cols]
        vc = v_scr[rows, cols]
        s = s_scr[h]
        sc = _dot_nt(qc, k_scr[rows, cols]) * dmask_ref[h]
        out = _dot(sc.astype(BF16), vc) + _dot(qc, s.astype(BF16)) * qdec_ref[h]
        s_scr[h] = s * cdec_ref[h] + _dot(kd_scr[rows, cols].T, vc)
        on = _group_norm(out, gn_ref[layer:layer + 1, cols])
        cr_ref[rows, CONV_CH + h * HEAD_DIM:CONV_CH + (h + 1) * HEAD_DIM] = (
            sg_scr[rows, cols] * on).astype(BF16)

    def out_proj(c):
        rows = slice(c * CHUNK, (c + 1) * CHUNK)
        m_ref[rows, :] = _dot(cr_ref[rows, :], wout_ref[...])

    glu(0)
    glu(1)
    stages = [proj_q, proj_k, proj_v, proj_gate]
    proj_stage = []
    for c in range(n_chunks):
        stages += [functools.partial(retention, c, h) for h in range(HEADS)]
        proj_stage.append(len(stages))
        stages.append(functools.partial(out_proj, c))
    n_blocks = tm // CONV_ROWS
    blocks_per_chunk = CHUNK // CONV_ROWS
    assert all((c + 1) * blocks_per_chunk <= proj_stage[c] for c in range(n_chunks))
    for p, stage in enumerate(stages):
        stage()
        if p < n_blocks:
            conv_block(p)

    ubuf[0:HIST, :] = ubuf[tm:tm + HIST, :]

    @pl.when(l == pl.num_programs(1) - 1)
    def _():
        convst_ref[...] = ubuf[CONV_OFF:HIST, :]
        retst_ref[...] = s_scr[...]


def _prompt_mix(x, w_in_b, w_out_b, tables, params, layer, decay, batch, seq, cast_weights):
    conv_w_rows, conv_b, conv_ln_g, conv_ln_b, ret_gn_g = params
    tm = PROMPT_TILE
    nl = seq // tm
    row = lambda b, l: (b * nl + l, 0)
    vec = lambda w: pl.BlockSpec((DEPTH, w), lambda b, l: (0, 0))
    tab_spec = pl.BlockSpec((tm, HEAD_DIM), lambda b, l: (l, 0))
    dec_spec = lambda rows, cols: pl.BlockSpec((HEADS, rows, cols), lambda b, l: (0, 0, 0))
    cast_in, cast_out, cast_shapes = _cast_specs(cast_weights, layer, batch * nl,
                                                 lambda b, l: b * nl + l)
    outs = pl.pallas_call(
        functools.partial(_prompt_mix_kernel, tm=tm, layer=layer),
        grid=(batch, nl),
        in_specs=[pl.BlockSpec((tm, D_MODEL), row),
                  pl.BlockSpec((None, D_MODEL, IN_COLS), lambda b, l: (0, 0, 0),
                               pipeline_mode=pl.Buffered(1)),
                  tab_spec, tab_spec, tab_spec, tab_spec,
                  pl.BlockSpec((None, CONV_K, SUBLANES, CONV_CH), lambda b, l: (layer, 0, 0, 0)),
                  vec(CONV_CH), vec(CONV_CH), vec(CONV_CH), vec(RET_W),
                  pl.BlockSpec((None, D_MODEL, D_MODEL), lambda b, l: (0, 0, 0),
                               pipeline_mode=pl.Buffered(1)),
                  dec_spec(CHUNK, CHUNK), dec_spec(CHUNK, HEAD_DIM), dec_spec(CHUNK, HEAD_DIM),
                  dec_spec(HEAD_DIM, HEAD_DIM)] + cast_in,
        out_specs=[pl.BlockSpec((tm, D_MODEL), row),
                   pl.BlockSpec((None, CONV_K - 1, CONV_CH), lambda b, l: (b, 0, 0)),
                   pl.BlockSpec((None, HEADS, HEAD_DIM, HEAD_DIM), lambda b, l: (b, 0, 0, 0))]
        + cast_out,
        out_shape=[jax.ShapeDtypeStruct((batch * seq, D_MODEL), F32),
                   jax.ShapeDtypeStruct((batch, CONV_K - 1, CONV_CH), F32),
                   jax.ShapeDtypeStruct((batch, HEADS, HEAD_DIM, HEAD_DIM), F32)] + cast_shapes,
        scratch_shapes=[pltpu.VMEM((HIST + tm, CONV_CH), F32),
                        pltpu.VMEM((SUBLANES - 1, HIST + tm - SUBLANES, CONV_CH), F32),
                        pltpu.VMEM((tm, D_MODEL), BF16),
                        pltpu.VMEM((HEADS, HEAD_DIM, HEAD_DIM), F32),
                        pltpu.VMEM((tm, RET_W), BF16),
                        pltpu.VMEM((tm, RET_W), BF16),
                        pltpu.VMEM((tm, RET_W), BF16),
                        pltpu.VMEM((tm, RET_W), BF16),
                        pltpu.VMEM((tm, RET_W), F32)],
        compiler_params=pltpu.CompilerParams(dimension_semantics=("arbitrary", "arbitrary"),
                                             vmem_limit_bytes=VMEM_LIMIT_BYTES),
        name="prompt_mix",
    )(x, w_in_b, *tables, conv_w_rows, conv_b, conv_ln_g, conv_ln_b, ret_gn_g, w_out_b, *decay,
      *cast_weights)
    return outs[0], outs[1], outs[2], tuple(outs[3:])


def _prompt_ffn_kernel(m_ref, x_ref, l1g_ref, l1b_ref, wgu_ref, wdn_ref, l2g_ref, l2b_ref,
                       *rest, tm, layer):
    n_cast = (len(rest) - 2) // 2
    o_ref, a_scr = rest[n_cast], rest[-1]
    _cast_slabs(rest[:n_cast], rest[n_cast + 1:-1])

    pm = tm // 2
    rows = (slice(0, pm), slice(pm, tm))
    l1g, l1b, l2g, l2b = (_row(r, layer) for r in (l1g_ref, l1b_ref, l2g_ref, l2b_ref))

    def norm1(p):
        return _layer_norm(DN_ALPHA * x_ref[rows[p], :] + m_ref[rows[p], :], l1g, l1b)

    def ff_chunk(p, xb, k):
        s, w = FF_CHUNKS[k]
        gate = _dot(xb, wgu_ref[:, s:s + w])
        up = _dot(xb, wgu_ref[:, D_FF + s:D_FF + s + w])
        a_scr[rows[p], s:s + w] = (_silu(gate) * up).astype(BF16)

    def down(p):
        return _dot(a_scr[rows[p], :], wdn_ref[...])

    def norm2(p, x1, f):
        o_ref[rows[p], :] = _layer_norm(DN_ALPHA * x1 + f, l2g, l2b)

    n_ff = len(FF_CHUNKS)
    x1_0 = norm1(0)
    xb0 = x1_0.astype(BF16)
    ff_chunk(0, xb0, 0)
    x1_1 = norm1(1)
    xb1 = x1_1.astype(BF16)
    for k in range(1, n_ff):
        ff_chunk(0, xb0, k)
    f0 = down(0)
    ff_chunk(1, xb1, 0)
    norm2(0, x1_0, f0)
    for k in range(1, n_ff):
        ff_chunk(1, xb1, k)
    norm2(1, x1_1, down(1))


def _prompt_ffn(m_mix, x, ln1_g, ln1_b, w_gu_b, w_dn_b, ln2_g, ln2_b, layer, next_weights):
    m = x.shape[0]
    tm = FFN_TILE
    n_steps = m // tm
    row = lambda i: (i, 0)
    fixed = lambda i: (0, 0, 0)
    once = pl.Buffered(1)
    vec = pl.BlockSpec((DEPTH, D_MODEL), lambda i: (0, 0))
    cast_in, cast_out, cast_shapes = _cast_specs(next_weights, layer + 1, n_steps, lambda i: i)
    outs = pl.pallas_call(
        functools.partial(_prompt_ffn_kernel, tm=tm, layer=layer),
        grid=(n_steps,),
        in_specs=[pl.BlockSpec((tm, D_MODEL), row),
                  pl.BlockSpec((tm, D_MODEL), row),
                  vec, vec,
                  pl.BlockSpec((None, D_MODEL, 2 * D_FF), fixed, pipeline_mode=once),
                  pl.BlockSpec((None, D_FF, D_MODEL), fixed, pipeline_mode=once),
                  vec, vec] + cast_in,
        out_specs=[pl.BlockSpec((tm, D_MODEL), row)] + cast_out,
        out_shape=[jax.ShapeDtypeStruct((m, D_MODEL), F32)] + cast_shapes,
        scratch_shapes=[pltpu.VMEM((tm, D_FF), BF16)],
        compiler_params=pltpu.CompilerParams(dimension_semantics=("arbitrary",),
                                             vmem_limit_bytes=VMEM_LIMIT_BYTES),
        name="prompt_ffn",
    )(m_mix, x, ln1_g, ln1_b, w_gu_b, w_dn_b, ln2_g, ln2_b, *next_weights)
    return outs[0], tuple(outs[1:])


def _mix_in_kernel(x_ref, w_ref, cq_ref, sq_ref, ck_ref, sk_ref,
                   u_ref, q_ref, k_ref, v_ref, sg_ref):
    xb = x_ref[...].astype(BF16)
    ca = _dot(xb, w_ref[:, 0:CONV_CH])
    cb = _dot(xb, w_ref[:, CONV_CH:2 * CONV_CH])
    u_ref[...] = ca * jax.nn.sigmoid(cb)
    o = 2 * CONV_CH
    zq = _dot(xb, w_ref[:, o:o + RET_W])
    zk = _dot(xb, w_ref[:, o + RET_W:o + 2 * RET_W])
    cq, sq, ck, sk = cq_ref[...], sq_ref[...], ck_ref[...], sk_ref[...]
    for h in range(HEADS):
        sl = slice(h * HEAD_DIM, (h + 1) * HEAD_DIM)
        q_ref[:, sl] = _rotary(zq[:, sl], cq, sq).astype(BF16)
        k_ref[:, sl] = _rotary(zk[:, sl], ck, sk)
    v_ref[...] = _dot(xb, w_ref[:, o + 2 * RET_W:o + 3 * RET_W]).astype(BF16)
    sg_ref[...] = _silu(_dot(xb, w_ref[:, o + 3 * RET_W:o + 4 * RET_W]))


def _mix_in(x, w_in_b, tables):
    m = x.shape[0]
    full = lambda w: pl.BlockSpec((m, w), lambda i: (0, 0))
    return pl.pallas_call(
        _mix_in_kernel,
        grid=(1,),
        in_specs=[full(D_MODEL),
                  pl.BlockSpec((None, D_MODEL, IN_COLS), lambda i: (0, 0, 0)),
                  full(HEAD_DIM), full(HEAD_DIM), full(HEAD_DIM), full(HEAD_DIM)],
        out_specs=[full(RET_W)] * 5,
        out_shape=[jax.ShapeDtypeStruct((m, CONV_CH), F32),
                   jax.ShapeDtypeStruct((m, RET_W), BF16),
                   jax.ShapeDtypeStruct((m, RET_W), F32),
                   jax.ShapeDtypeStruct((m, RET_W), BF16),
                   jax.ShapeDtypeStruct((m, RET_W), F32)],
        compiler_params=pltpu.CompilerParams(dimension_semantics=("arbitrary",),
                                             vmem_limit_bytes=VMEM_LIMIT_BYTES),
        name="mix_in",
    )(x, w_in_b, *tables)


def _mix_core_sample_kernel(*refs, tb, aliased, layer):
    if aliased:
        refs = refs[:18] + refs[20:]
    (u_ref, q_ref, k_ref, v_ref, sg_ref, x_ref, cst_ref, rst_ref,
     cw_ref, cb_ref, clg_ref, clb_ref, gn_ref, wout_ref, l1g_ref, l1b_ref,
     kdec_ref, cdec_ref,
     x1_ref, cso_ref, rso_ref,
     o_scr) = refs

    u = u_ref[...]
    conv = u * cw_ref[CONV_K - 1:CONV_K, :] + _row(cb_ref, layer)
    for j in range(CONV_K - 1):
        conv = conv + cst_ref[j] * cw_ref[j:j + 1, :]
        if j > 0:
            cso_ref[j - 1] = cst_ref[j]
    cso_ref[CONV_K - 2] = u
    c = _silu(_layer_norm(conv, _row(clg_ref, layer), _row(clb_ref, layer))).astype(BF16)

    wide = tb * HEAD_DIM
    in_block = (lax.broadcasted_iota(jnp.int32, (tb, wide), 1) // HEAD_DIM
                == lax.broadcasted_iota(jnp.int32, (tb, wide), 0))
    ones_bd = jnp.where(in_block, 1.0, 0.0).astype(BF16)
    contract_rows = (((0,), (0,)), ((), ()))
    for h in range(HEADS):
        cols = slice(h * HEAD_DIM, (h + 1) * HEAD_DIM)
        kd = (k_ref[:, cols] * kdec_ref[h, 0:tb, :]).astype(BF16)
        v_bd = jnp.where(in_block, jnp.concatenate([v_ref[:, cols].astype(F32)] * tb, axis=1),
                         0.0).astype(BF16)
        kv = lax.dot_general(kd, v_bd, contract_rows, preferred_element_type=F32)
        q_cols = lax.dot_general(q_ref[:, cols], ones_bd, contract_rows,
                                 preferred_element_type=F32)
        for b in range(tb):
            blk = slice(b * HEAD_DIM, (b + 1) * HEAD_DIM)
            s_new = rst_ref[b, h] * cdec_ref[h] + kv[:, blk]
            rso_ref[b, h] = s_new
            o_scr[b:b + 1, cols] = jnp.sum(q_cols[:, blk] * s_new, axis=0, keepdims=True)

    rs = []
    for h in range(HEADS):
        cols = slice(h * HEAD_DIM, (h + 1) * HEAD_DIM)
        on = _group_norm(o_scr[:, cols], gn_ref[layer:layer + 1, cols])
        rs.append((sg_ref[:, cols] * on).astype(BF16))
    cr = jnp.concatenate([c] + rs, axis=-1)
    m = _dot(cr, wout_ref[...])
    x1_ref[...] = _layer_norm(DN_ALPHA * x_ref[...] + m, _row(l1g_ref, layer), _row(l1b_ref, layer))


def _mix_core_sample(u, q, k, v, sg, x, state_conv, state_ret, params, layer, decay, bufs):
    conv_w, conv_b, conv_ln_g, conv_ln_b, ret_gn_g, w_out_b, ln1_g, ln1_b = params
    nb = x.shape[0]
    tb = SAMPLE_TILE
    row = lambda i: (i, 0)
    lay2 = lambda i: (layer, 0, 0)
    act = lambda w: pl.BlockSpec((tb, w), row)
    vec = lambda w: pl.BlockSpec((DEPTH, w), lambda i: (0, 0))
    dec_spec = pl.BlockSpec((HEADS, HEAD_DIM, HEAD_DIM), lambda i: (0, 0, 0))
    cst_spec = pl.BlockSpec((None, CONV_K - 1, tb, CONV_CH), lambda i: (layer, 0, i, 0))
    rst_spec = pl.BlockSpec((None, tb, HEADS, HEAD_DIM, HEAD_DIM), lambda i: (layer, i, 0, 0, 0))
    in_specs = [act(CONV_CH), act(RET_W), act(RET_W), act(RET_W), act(RET_W), act(D_MODEL),
                cst_spec, rst_spec,
                pl.BlockSpec((None, CONV_K, CONV_CH), lay2),
                vec(CONV_CH), vec(CONV_CH), vec(CONV_CH), vec(RET_W),
                pl.BlockSpec((None, D_MODEL, D_MODEL), lambda i: (0, 0, 0)),
                vec(D_MODEL), vec(D_MODEL),
                dec_spec, dec_spec]
    args = [u, q, k, v, sg, x, state_conv, state_ret,
            conv_w, conv_b, conv_ln_g, conv_ln_b, ret_gn_g, w_out_b, ln1_g, ln1_b, *decay]
    aliases = {}
    if bufs is not None:
        in_specs += [pl.BlockSpec(memory_space=pl.ANY), pl.BlockSpec(memory_space=pl.ANY)]
        args += list(bufs)
        aliases = {18: 1, 19: 2}
    return pl.pallas_call(
        functools.partial(_mix_core_sample_kernel, tb=tb, aliased=bufs is not None, layer=layer),
        grid=(nb // tb,),
        in_specs=in_specs,
        out_specs=[act(D_MODEL), cst_spec, rst_spec],
        out_shape=[jax.ShapeDtypeStruct((nb, D_MODEL), F32),
                   jax.ShapeDtypeStruct(state_conv.shape, F32),
                   jax.ShapeDtypeStruct(state_ret.shape, F32)],
        scratch_shapes=[pltpu.VMEM((tb, RET_W), F32)],
        input_output_aliases=aliases,
        compiler_params=pltpu.CompilerParams(dimension_semantics=("arbitrary",),
                                             vmem_limit_bytes=VMEM_LIMIT_BYTES),
        name="mix_core_sample",
    )(*args)


def _ffn_kernel(x_ref, wgu_ref, wdn_ref, g_ref, b_ref, o_ref, a_scr, *, layer):
    x = x_ref[...]
    f = _swiglu_ffn(x, wgu_ref, wdn_ref, a_scr)
    o_ref[...] = _layer_norm(DN_ALPHA * x + f, _row(g_ref, layer), _row(b_ref, layer))


def _ffn(x, w_gu_b, w_dn_b, ln2_g, ln2_b, layer):
    m = x.shape[0]
    once = pl.Buffered(1)
    vec = pl.BlockSpec((DEPTH, D_MODEL), lambda i: (0, 0))
    return pl.pallas_call(
        functools.partial(_ffn_kernel, layer=layer),
        grid=(1,),
        in_specs=[pl.BlockSpec((m, D_MODEL), lambda i: (0, 0)),
                  pl.BlockSpec((None, D_MODEL, 2 * D_FF), lambda i: (0, 0, 0), pipeline_mode=once),
                  pl.BlockSpec((None, D_FF, D_MODEL), lambda i: (0, 0, 0), pipeline_mode=once),
                  vec, vec],
        out_specs=pl.BlockSpec((m, D_MODEL), lambda i: (0, 0)),
        out_shape=jax.ShapeDtypeStruct((m, D_MODEL), F32),
        scratch_shapes=[pltpu.VMEM((m, D_FF), BF16)],
        compiler_params=pltpu.CompilerParams(dimension_semantics=("arbitrary",),
                                             vmem_limit_bytes=VMEM_LIMIT_BYTES),
        name="ffn",
    )(x, w_gu_b, w_dn_b, ln2_g, ln2_b)


def _rotary_tables(pos, rows):
    half = HEAD_DIM // 2
    inv_freq = ROPE_BASE ** (-np.arange(0, half, dtype=np.float64) / half)
    ang = np.asarray(pos, np.float64)[:, None] * inv_freq[None, :]
    cos, sin = np.cos(ang), np.sin(ang)
    cos2 = np.concatenate([cos, cos], axis=-1)
    sin2 = np.concatenate([-sin, sin], axis=-1)
    scale = HEAD_DIM ** -0.5
    return tuple(jnp.asarray(np.broadcast_to(t, (rows, HEAD_DIM)).astype(np.float32))
                 for t in (cos2, sin2, cos2 * scale, sin2 * scale))


def _decay_tables(chunk):
    log_g = np.log(1.0 - 2.0 ** (-5.0 - np.arange(HEADS, dtype=np.float64)))
    idx = np.arange(chunk, dtype=np.float64)
    diff = idx[:, None] - idx[None, :]
    dmask = np.where(diff[None] >= 0,
                     np.exp(np.maximum(diff, 0.0)[None] * log_g[:, None, None]), 0.0)
    q_decay = np.exp((idx + 1.0)[None, :] * log_g[:, None])
    k_decay = np.exp((chunk - 1.0 - idx)[None, :] * log_g[:, None])
    chunk_decay = np.exp(chunk * log_g)
    return tuple(t.astype(np.float32) for t in (dmask, q_decay, k_decay, chunk_decay))


def kernel(x_prompt, x_sample, state_conv, state_ret, w_in, conv_w, conv_b, conv_ln_g, conv_ln_b,
           ret_gn_g, w_out, ln1_g, ln1_b, w_gate_up, w_down, ln2_g, ln2_b):
    batch, seq, _ = x_prompt.shape
    nb = x_sample.shape[0]
    weights_f32 = (w_in, w_out, w_gate_up, w_down)
    weights_b = (w_in[0:1].astype(BF16), w_out[0:1].astype(BF16))
    conv_params = (conv_w, conv_b, conv_ln_g, conv_ln_b, ret_gn_g)
    conv_w_rows = jnp.broadcast_to(conv_w[:, :, None, :], (DEPTH, CONV_K, SUBLANES, CONV_CH))

    tab_p = _rotary_tables(np.arange(seq), seq)
    tab_s = _rotary_tables(PAST_LEN + np.arange(1), nb)

    dmask, q_dec, k_dec, c_dec = _decay_tables(CHUNK)
    bc = lambda a: jnp.asarray(np.broadcast_to(a, (HEADS, CHUNK, HEAD_DIM)))
    bc1 = lambda a: jnp.asarray(np.broadcast_to(a, (HEADS, HEAD_DIM, HEAD_DIM)))
    decay_p = (jnp.asarray(dmask), bc(q_dec[:, :, None]), bc(k_dec[:, :, None]),
               bc1(c_dec[:, None, None]))
    _, _, k_dec1, c_dec1 = _decay_tables(1)
    decay_s = (bc1(k_dec1[:, :, None]), bc1(c_dec1[:, None, None]))

    xp = x_prompt.reshape(batch * seq, D_MODEL)
    xs = x_sample.reshape(nb, D_MODEL)
    conv_p, ret_p = [], []
    bufs = None
    state_conv_t = jnp.transpose(state_conv, (0, 2, 1, 3))
    for layer in range(DEPTH):
        m_mix, cst, rst, ffn_b = _prompt_mix(
            xp, weights_b[0], weights_b[1], tab_p, (conv_w_rows,) + conv_params[1:], layer, decay_p,
            batch, seq, weights_f32[2:] if layer == 0 else ())
        w_in_b, w_out_b, w_gu_b, w_dn_b = weights_b + ffn_b
        conv_p.append(cst)
        ret_p.append(rst)
        xp, weights_b = _prompt_ffn(m_mix, xp, ln1_g, ln1_b, w_gu_b, w_dn_b, ln2_g, ln2_b,
                                    layer, weights_f32 if layer + 1 < DEPTH else ())

        u, q, k, v, sg = _mix_in(xs, w_in_b, tab_s)
        x1, cso, rso = _mix_core_sample(u, q, k, v, sg, xs, state_conv_t, state_ret,
                                        conv_params + (w_out_b, ln1_g, ln1_b), layer, decay_s, bufs)
        bufs = (cso, rso)
        xs = _ffn(x1, w_gu_b, w_dn_b, ln2_g, ln2_b, layer)

    return (xp.reshape(batch, seq, D_MODEL), xs.reshape(nb, 1, D_MODEL),
            jnp.stack(conv_p, axis=0), jnp.stack(ret_p, axis=0),
            jnp.transpose(bufs[0], (0, 2, 1, 3)), bufs[1])
```

```python
import functools

import jax
import jax.numpy as jnp
import numpy as np
from jax import lax
from jax.experimental import pallas as pl
from jax.experimental.pallas import tpu as pltpu

F32 = jnp.float32
BF16 = jnp.bfloat16

D_MODEL = 1024
DEPTH = 4
CONV_CH = 512
CONV_K = 31
HEADS = 4
HEAD_DIM = 128
RET_W = HEADS * HEAD_DIM
D_FF = 2816
IN_COLS = 2 * CONV_CH + 4 * RET_W
PAST_LEN = 16384
ROPE_BASE = 10000.0
DN_ALPHA = (2 * DEPTH) ** 0.25
LN_EPS = 1e-5
CHUNK = 256

SUBLANES = 8
VMEM_LIMIT_BYTES = 56 * 1024 * 1024

HIST = 32
CONV_OFF = HIST - (CONV_K - 1)
CONV_ROWS = 64
FF_CHUNK_W = 256
FF_CHUNKS = tuple((s, min(FF_CHUNK_W, D_FF - s)) for s in range(0, D_FF, FF_CHUNK_W))
CAST_STEPS = 2

PROMPT_TILE = 512
FFN_TILE = 512
SAMPLE_TILE = 16


def _dot(a, b):
    return jnp.dot(a, b, preferred_element_type=F32)


def _dot_nt(a, b):
    return lax.dot_general(a, b, (((1,), (1,)), ((), ())), preferred_element_type=F32)


def _layer_norm(y, g, b):
    mu = jnp.mean(y, axis=-1, keepdims=True)
    d = y - mu
    var = jnp.mean(d * d, axis=-1, keepdims=True)
    return d * lax.rsqrt(var + LN_EPS) * g + b


def _group_norm(o, g):
    mu = jnp.mean(o, axis=-1, keepdims=True)
    d = o - mu
    var = jnp.mean(d * d, axis=-1, keepdims=True)
    return d * lax.rsqrt(var + LN_EPS) * g


def _silu(x):
    return x * jax.nn.sigmoid(x)


def _row(ref, layer):
    return ref[layer:layer + 1, :]


def _rotary(seg, cos, sin_signed):
    return seg * cos + pltpu.roll(seg, HEAD_DIM // 2, 1) * sin_signed


def _shift_copies(win_ref, sh_ref, lo, hi):
    for r in range(1, SUBLANES):
        sh_ref[r - 1, lo:hi, :] = win_ref[r + lo:r + hi, :]


def _conv_block(win_ref, sh_ref, cw_ref, cbias, clg, clb, r0):
    acc = jnp.broadcast_to(cbias, (CONV_ROWS, CONV_CH))
    for j in range(CONV_K):
        a, r = divmod(CONV_OFF + j, SUBLANES)
        lo = r0 + a * SUBLANES
        src = win_ref[lo:lo + CONV_ROWS, :] if r == 0 else sh_ref[r - 1, lo:lo + CONV_ROWS, :]
        acc = acc + src * jnp.concatenate([cw_ref[j]] * (CONV_ROWS // SUBLANES), axis=0)
    return _silu(_layer_norm(acc, clg, clb)).astype(BF16)


def _cast_specs(weights, layer, n_steps, step):
    n_slabs = n_steps // CAST_STEPS
    ins, outs, shapes = [], [], []
    for w in weights:
        _, rows, cols = w.shape
        slab = rows // n_slabs
        ins.append(pl.BlockSpec((None, slab, cols), lambda *g: (layer, step(*g) // CAST_STEPS, 0)))
        outs.append(pl.BlockSpec((None, slab, cols), lambda *g: (0, step(*g) // CAST_STEPS, 0)))
        shapes.append(jax.ShapeDtypeStruct((1, rows, cols), BF16))
    return ins, outs, shapes


def _cast_slabs(src_refs, dst_refs):
    for src_ref, dst_ref in zip(src_refs, dst_refs):
        dst_ref[...] = src_ref[...].astype(BF16)


def _prompt_mix_kernel(x_ref, w_ref, cq_ref, sq_ref, ck_ref, sk_ref,
                       cw_ref, cb_ref, clg_ref, clb_ref, gn_ref, wout_ref,
                       dmask_ref, qdec_ref, kdec_ref, cdec_ref, *rest, tm, layer):
    n_cast = (len(rest) - 12) // 2
    m_ref, convst_ref, retst_ref = rest[n_cast:n_cast + 3]
    ubuf, shbuf, cr_ref, s_scr, q_scr, k_scr, kd_scr, v_scr, sg_scr = rest[2 * n_cast + 3:]
    l = pl.program_id(1)

    @pl.when(l == 0)
    def _():
        ubuf[0:HIST, :] = jnp.zeros((HIST, CONV_CH), F32)
        s_scr[...] = jnp.zeros_like(s_scr)

    _cast_slabs(rest[:n_cast], rest[n_cast + 3:2 * n_cast + 3])
    xb = x_ref[...].astype(BF16)
    n_chunks = tm // CHUNK

    half = tm // 2
    halo = HIST - SUBLANES

    def glu(p):
        rows = slice(p * half, (p + 1) * half)
        ca = _dot(xb[rows, :], w_ref[:, 0:CONV_CH])
        cb = _dot(xb[rows, :], w_ref[:, CONV_CH:2 * CONV_CH])
        ubuf[HIST + p * half:HIST + (p + 1) * half, :] = ca * jax.nn.sigmoid(cb)
        _shift_copies(ubuf, shbuf, p * half + (halo if p else 0), (p + 1) * half + halo)

    clg, clb, cbias = _row(clg_ref, layer), _row(clb_ref, layer), _row(cb_ref, layer)

    def conv_block(rb):
        r0 = rb * CONV_ROWS
        cr_ref[r0:r0 + CONV_ROWS, 0:CONV_CH] = _conv_block(ubuf, shbuf, cw_ref, cbias, clg, clb, r0)

    o = 2 * CONV_CH
    cq, sq, ck, sk = cq_ref[...], sq_ref[...], ck_ref[...], sk_ref[...]

    def proj_q():
        zq = _dot(xb, w_ref[:, o:o + RET_W])
        for h in range(HEADS):
            cols = slice(h * HEAD_DIM, (h + 1) * HEAD_DIM)
            q_scr[:, cols] = _rotary(zq[:, cols], cq, sq).astype(BF16)

    def proj_k():
        zk = _dot(xb, w_ref[:, o + RET_W:o + 2 * RET_W])
        for h in range(HEADS):
            cols = slice(h * HEAD_DIM, (h + 1) * HEAD_DIM)
            kr = _rotary(zk[:, cols], ck, sk)
            k_scr[:, cols] = kr.astype(BF16)
            for c in range(n_chunks):
                rows = slice(c * CHUNK, (c + 1) * CHUNK)
                kd_scr[rows, cols] = (kr[rows, :] * kdec_ref[h]).astype(BF16)

    def proj_v():
        v_scr[...] = _dot(xb, w_ref[:, o + 2 * RET_W:o + 3 * RET_W]).astype(BF16)

    def proj_gate():
        sg_scr[...] = _silu(_dot(xb, w_ref[:, o + 3 * RET_W:o + 4 * RET_W]))

    def retention(c, h):
        rows = slice(c * CHUNK, (c + 1) * CHUNK)
        cols = slice(h * HEAD_DIM, (h + 1) * HEAD_DIM)
        qc = q_scr[rows, cols]
        vc = v_scr[rows, cols]
        s = s_scr[h]
        sc = _dot_nt(qc, k_scr[rows, cols]) * dmask_ref[h]
        out = _dot(sc.astype(BF16), vc) + _dot(qc, s.astype(BF16)) * qdec_ref[h]
        s_scr[h] = s * cdec_ref[h] + _dot(kd_scr[rows, cols].T, vc)
        on = _group_norm(out, gn_ref[layer:layer + 1, cols])
        cr_ref[rows, CONV_CH + h * HEAD_DIM:CONV_CH + (h + 1) * HEAD_DIM] = (
            sg_scr[rows, cols] * on).astype(BF16)

    def out_proj(c):
        rows = slice(c * CHUNK, (c + 1) * CHUNK)
        m_ref[rows, :] = _dot(cr_ref[rows, :], wout_ref[...])

    glu(0)
    glu(1)
    stages = [proj_q, proj_k, proj_v, proj_gate]
    proj_stage = []
    for c in range(n_chunks):
        stages += [functools.partial(retention, c, h) for h in range(HEADS)]
        proj_stage.append(len(stages))
        stages.append(functools.partial(out_proj, c))
    n_blocks = tm // CONV_ROWS
    blocks_per_chunk = CHUNK // CONV_ROWS
    assert all((c + 1) * blocks_per_chunk <= proj_stage[c] for c in range(n_chunks))
    for p, stage in enumerate(stages):
        stage()
        if p < n_blocks:
            conv_block(p)

    ubuf[0:HIST, :] = ubuf[tm:tm + HIST, :]

    @pl.when(l == pl.num_programs(1) - 1)
    def _():
        convst_ref[...] = ubuf[CONV_OFF:HIST, :]
        retst_ref[...] = s_scr[...]


def _prompt_mix(x, w_in_b, w_out_b, tables, params, layer, decay, batch, seq, cast_weights):
    conv_w_rows, conv_b, conv_ln_g, conv_ln_b, ret_gn_g = params
    tm = PROMPT_TILE
    nl = seq // tm
    row = lambda b, l: (b * nl + l, 0)
    vec = lambda w: pl.BlockSpec((DEPTH, w), lambda b, l: (0, 0))
    tab_spec = pl.BlockSpec((tm, HEAD_DIM), lambda b, l: (l, 0))
    dec_spec = lambda rows, cols: pl.BlockSpec((HEADS, rows, cols), lambda b, l: (0, 0, 0))
    cast_in, cast_out, cast_shapes = _cast_specs(cast_weights, layer, batch * nl,
                                                 lambda b, l: b * nl + l)
    outs = pl.pallas_call(
        functools.partial(_prompt_mix_kernel, tm=tm, layer=layer),
        grid=(batch, nl),
        in_specs=[pl.BlockSpec((tm, D_MODEL), row),
                  pl.BlockSpec((None, D_MODEL, IN_COLS), lambda b, l: (0, 0, 0),
                               pipeline_mode=pl.Buffered(1)),
                  tab_spec, tab_spec, tab_spec, tab_spec,
                  pl.BlockSpec((None, CONV_K, SUBLANES, CONV_CH), lambda b, l: (layer, 0, 0, 0)),
                  vec(CONV_CH), vec(CONV_CH), vec(CONV_CH), vec(RET_W),
                  pl.BlockSpec((None, D_MODEL, D_MODEL), lambda b, l: (0, 0, 0),
                               pipeline_mode=pl.Buffered(1)),
                  dec_spec(CHUNK, CHUNK), dec_spec(CHUNK, HEAD_DIM), dec_spec(CHUNK, HEAD_DIM),
                  dec_spec(HEAD_DIM, HEAD_DIM)] + cast_in,
        out_specs=[pl.BlockSpec((tm, D_MODEL), row),
                   pl.BlockSpec((None, CONV_K - 1, CONV_CH), lambda b, l: (b, 0, 0)),
                   pl.BlockSpec((None, HEADS, HEAD_DIM, HEAD_DIM), lambda b, l: (b, 0, 0, 0))]
        + cast_out,
        out_shape=[jax.ShapeDtypeStruct((batch * seq, D_MODEL), F32),
                   jax.ShapeDtypeStruct((batch, CONV_K - 1, CONV_CH), F32),
                   jax.ShapeDtypeStruct((batch, HEADS, HEAD_DIM, HEAD_DIM), F32)] + cast_shapes,
        scratch_shapes=[pltpu.VMEM((HIST + tm, CONV_CH), F32),
                        pltpu.VMEM((SUBLANES - 1, HIST + tm - SUBLANES, CONV_CH), F32),
                        pltpu.VMEM((tm, D_MODEL), BF16),
                        pltpu.VMEM((HEADS, HEAD_DIM, HEAD_DIM), F32),
                        pltpu.VMEM((tm, RET_W), BF16),
                        pltpu.VMEM((tm, RET_W), BF16),
                        pltpu.VMEM((tm, RET_W), BF16),
                        pltpu.VMEM((tm, RET_W), BF16),
                        pltpu.VMEM((tm, RET_W), F32)],
        compiler_params=pltpu.CompilerParams(dimension_semantics=("arbitrary", "arbitrary"),
                                             vmem_limit_bytes=VMEM_LIMIT_BYTES),
        name="prompt_mix",
    )(x, w_in_b, *tables, conv_w_rows, conv_b, conv_ln_g, conv_ln_b, ret_gn_g, w_out_b, *decay,
      *cast_weights)
    return outs[0], outs[1], outs[2], tuple(outs[3:])


def _prompt_ffn_kernel(m_ref, x_ref, l1g_ref, l1b_ref, wgu_ref, wdn_ref, l2g_ref, l2b_ref,
                       *rest, tm, layer):
    n_cast = (len(rest) - 2) // 2
    o_ref, a_scr = rest[n_cast], rest[-1]
    _cast_slabs(rest[:n_cast], rest[n_cast + 1:-1])

    pm = tm // 2
    rows = (slice(0, pm), slice(pm, tm))
    l1g, l1b, l2g, l2b = (_row(r, layer) for r in (l1g_ref, l1b_ref, l2g_ref, l2b_ref))

    def norm1(p):
        return _layer_norm(DN_ALPHA * x_ref[rows[p], :] + m_ref[rows[p], :], l1g, l1b)

    def ff_chunk(p, xb, k):
        s, w = FF_CHUNKS[k]
        gate = _dot(xb, wgu_ref[:, s:s + w])
        up = _dot(xb, wgu_ref[:, D_FF + s:D_FF + s + w])
        a_scr[rows[p], s:s + w] = (_silu(gate) * up).astype(BF16)

    def down(p):
        return _dot(a_scr[rows[p], :], wdn_ref[...])

    def norm2(p, x1, f):
        o_ref[rows[p], :] = _layer_norm(DN_ALPHA * x1 + f, l2g, l2b)

    n_ff = len(FF_CHUNKS)
    x1_0 = norm1(0)
    xb0 = x1_0.astype(BF16)
    ff_chunk(0, xb0, 0)
    x1_1 = norm1(1)
    xb1 = x1_1.astype(BF16)
    for k in range(1, n_ff):
        ff_chunk(0, xb0, k)
    f0 = down(0)
    ff_chunk(1, xb1, 0)
    norm2(0, x1_0, f0)
    for k in range(1, n_ff):
        ff_chunk(1, xb1, k)
    norm2(1, x1_1, down(1))


def _prompt_ffn(m_mix, x, ln1_g, ln1_b, w_gu_b, w_dn_b, ln2_g, ln2_b, layer, next_weights):
    m = x.shape[0]
    tm = FFN_TILE
    n_steps = m // tm
    row = lambda i: (i, 0)
    fixed = lambda i: (0, 0, 0)
    once = pl.Buffered(1)
    vec = pl.BlockSpec((DEPTH, D_MODEL), lambda i: (0, 0))
    cast_in, cast_out, cast_shapes = _cast_specs(next_weights, layer + 1, n_steps, lambda i: i)
    outs = pl.pallas_call(
        functools.partial(_prompt_ffn_kernel, tm=tm, layer=layer),
        grid=(n_steps,),
        in_specs=[pl.BlockSpec((tm, D_MODEL), row),
                  pl.BlockSpec((tm, D_MODEL), row),
                  vec, vec,
                  pl.BlockSpec((None, D_MODEL, 2 * D_FF), fixed, pipeline_mode=once),
                  pl.BlockSpec((None, D_FF, D_MODEL), fixed, pipeline_mode=once),
                  vec, vec] + cast_in,
        out_specs=[pl.BlockSpec((tm, D_MODEL), row)] + cast_out,
        out_shape=[jax.ShapeDtypeStruct((m, D_MODEL), F32)] + cast_shapes,
        scratch_shapes=[pltpu.VMEM((tm, D_FF), BF16)],
        compiler_params=pltpu.CompilerParams(dimension_semantics=("arbitrary",),
                                             vmem_limit_bytes=VMEM_LIMIT_BYTES),
        name="prompt_ffn",
    )(m_mix, x, ln1_g, ln1_b, w_gu_b, w_dn_b, ln2_g, ln2_b, *next_weights)
    return outs[0], tuple(outs[1:])


def _mix_in_kernel(x_ref, w_ref, cq_ref, sq_ref, ck_ref, sk_ref,
                   u_ref, q_ref, k_ref, v_ref, sg_ref):
    xb = x_ref[...].astype(BF16)
    ca = _dot(xb, w_ref[:, 0:CONV_CH])
    cb = _dot(xb, w_ref[:, CONV_CH:2 * CONV_CH])
    u_ref[...] = ca * jax.nn.sigmoid(cb)
    o = 2 * CONV_CH
    zq = _dot(xb, w_ref[:, o:o + RET_W])
    zk = _dot(xb, w_ref[:, o + RET_W:o + 2 * RET_W])
    cq, sq, ck, sk = cq_ref[...], sq_ref[...], ck_ref[...], sk_ref[...]
    for h in range(HEADS):
        sl = slice(h * HEAD_DIM, (h + 1) * HEAD_DIM)
        q_ref[:, sl] = _rotary(zq[:, sl], cq, sq).astype(BF16)
        k_ref[:, sl] = _rotary(zk[:, sl], ck, sk)
    v_ref[...] = _dot(xb, w_ref[:, o + 2 * RET_W:o + 3 * RET_W]).astype(BF16)
    sg_ref[...] = _silu(_dot(xb, w_ref[:, o + 3 * RET_W:o + 4 * RET_W]))


def _mix_in(x, w_in_b, tables):
    m = x.shape[0]
    full = lambda w: pl.BlockSpec((m, w), lambda i: (0, 0))
    return pl.pallas_call(
        _mix_in_kernel,
        grid=(1,),
        in_specs=[full(D_MODEL),
                  pl.BlockSpec((None, D_MODEL, IN_COLS), lambda i: (0, 0, 0)),
                  full(HEAD_DIM), full(HEAD_DIM), full(HEAD_DIM), full(HEAD_DIM)],
        out_specs=[full(RET_W)] * 5,
        out_shape=[jax.ShapeDtypeStruct((m, CONV_CH), F32),
                   jax.ShapeDtypeStruct((m, RET_W), BF16),
                   jax.ShapeDtypeStruct((m, RET_W), F32),
                   jax.ShapeDtypeStruct((m, RET_W), BF16),
                   jax.ShapeDtypeStruct((m, RET_W), F32)],
        compiler_params=pltpu.CompilerParams(dimension_semantics=("arbitrary",),
                                             vmem_limit_bytes=VMEM_LIMIT_BYTES),
        name="mix_in",
    )(x, w_in_b, *tables)


def _mix_core_sample_kernel(*refs, tb, aliased, layer):
    if aliased:
        refs = refs[:18] + refs[20:]
    (u_ref, q_ref, k_ref, v_ref, sg_ref, x_ref, cst_ref, rst_ref,
     cw_ref, cb_ref, clg_ref, clb_ref, gn_ref, wout_ref, l1g_ref, l1b_ref,
     kdec_ref, cdec_ref,
     x1_ref, cso_ref, rso_ref,
     o_scr) = refs

    u = u_ref[...]
    conv = u * cw_ref[CONV_K - 1:CONV_K, :] + _row(cb_ref, layer)
    for j in range(CONV_K - 1):
        conv = conv + cst_ref[j] * cw_ref[j:j + 1, :]
        if j > 0:
            cso_ref[j - 1] = cst_ref[j]
    cso_ref[CONV_K - 2] = u
    c = _silu(_layer_norm(conv, _row(clg_ref, layer), _row(clb_ref, layer))).astype(BF16)

    wide = tb * HEAD_DIM
    in_block = (lax.broadcasted_iota(jnp.int32, (tb, wide), 1) // HEAD_DIM
                == lax.broadcasted_iota(jnp.int32, (tb, wide), 0))
    ones_bd = jnp.where(in_block, 1.0, 0.0).astype(BF16)
    contract_rows = (((0,), (0,)), ((), ()))
    for h in range(HEADS):
        cols = slice(h * HEAD_DIM, (h + 1) * HEAD_DIM)
        kd = (k_ref[:, cols] * kdec_ref[h, 0:tb, :]).astype(BF16)
        v_bd = jnp.where(in_block, jnp.concatenate([v_ref[:, cols].astype(F32)] * tb, axis=1),
                         0.0).astype(BF16)
        kv = lax.dot_general(kd, v_bd, contract_rows, preferred_element_type=F32)
        q_cols = lax.dot_general(q_ref[:, cols], ones_bd, contract_rows,
                                 preferred_element_type=F32)
        for b in range(tb):
            blk = slice(b * HEAD_DIM, (b + 1) * HEAD_DIM)
            s_new = rst_ref[b, h] * cdec_ref[h] + kv[:, blk]
            rso_ref[b, h] = s_new
            o_scr[b:b + 1, cols] = jnp.sum(q_cols[:, blk] * s_new, axis=0, keepdims=True)

    rs = []
    for h in range(HEADS):
        cols = slice(h * HEAD_DIM, (h + 1) * HEAD_DIM)
        on = _group_norm(o_scr[:, cols], gn_ref[layer:layer + 1, cols])
        rs.append((sg_ref[:, cols] * on).astype(BF16))
    cr = jnp.concatenate([c] + rs, axis=-1)
    m = _dot(cr, wout_ref[...])
    x1_ref[...] = _layer_norm(DN_ALPHA * x_ref[...] + m, _row(l1g_ref, layer), _row(l1b_ref, layer))


def _mix_core_sample(u, q, k, v, sg, x, state_conv, state_ret, params, layer, decay, bufs):
    conv_w, conv_b, conv_ln_g, conv_ln_b, ret_gn_g, w_out_b, ln1_g, ln1_b = params
    nb = x.shape[0]
    tb = SAMPLE_TILE
    row = lambda i: (i, 0)
    lay2 = lambda i: (layer, 0, 0)
    act = lambda w: pl.BlockSpec((tb, w), row)
    vec = lambda w: pl.BlockSpec((DEPTH, w), lambda i: (0, 0))
    dec_spec = pl.BlockSpec((HEADS, HEAD_DIM, HEAD_DIM), lambda i: (0, 0, 0))
    cst_spec = pl.BlockSpec((None, CONV_K - 1, tb, CONV_CH), lambda i: (layer, 0, i, 0))
    rst_spec = pl.BlockSpec((None, tb, HEADS, HEAD_DIM, HEAD_DIM), lambda i: (layer, i, 0, 0, 0))
    in_specs = [act(CONV_CH), act(RET_W), act(RET_W), act(RET_W), act(RET_W), act(D_MODEL),
                cst_spec, rst_spec,
                pl.BlockSpec((None, CONV_K, CONV_CH), lay2),
                vec(CONV_CH), vec(CONV_CH), vec(CONV_CH), vec(RET_W),
                pl.BlockSpec((None, D_MODEL, D_MODEL), lambda i: (0, 0, 0)),
                vec(D_MODEL), vec(D_MODEL),
                dec_spec, dec_spec]
    args = [u, q, k, v, sg, x, state_conv, state_ret,
            conv_w, conv_b, conv_ln_g, conv_ln_b, ret_gn_g, w_out_b, ln1_g, ln1_b, *decay]
    aliases = {}
    if bufs is not None:
        in_specs += [pl.BlockSpec(memory_space=pl.ANY), pl.BlockSpec(memory_space=pl.ANY)]
        args += list(bufs)
        aliases = {18: 1, 19: 2}
    return pl.pallas_call(
        functools.partial(_mix_core_sample_kernel, tb=tb, aliased=bufs is not None, layer=layer),
        grid=(nb // tb,),
        in_specs=in_specs,
        out_specs=[act(D_MODEL), cst_spec, rst_spec],
        out_shape=[jax.ShapeDtypeStruct((nb, D_MODEL), F32),
                   jax.ShapeDtypeStruct(state_conv.shape, F32),
                   jax.ShapeDtypeStruct(state_ret.shape, F32)],
        scratch_shapes=[pltpu.VMEM((tb, RET_W), F32)],
        input_output_aliases=aliases,
        compiler_params=pltpu.CompilerParams(dimension_semantics=("arbitrary",),
                                             vmem_limit_bytes=VMEM_LIMIT_BYTES),
        name="mix_core_sample",
    )(*args)


def _ffn_kernel(x_ref, wg_ref, wu_ref, wd_ref, g_ref, b_ref, o_ref, xb_scr, acc_scr, *, layer):
    k = pl.program_id(0)

    @pl.when(k == 0)
    def _():
        xb_scr[...] = x_ref[...].astype(BF16)
        acc_scr[...] = jnp.zeros_like(acc_scr)

    xb = xb_scr[...]
    act = (_silu(_dot(xb, wg_ref[...])) * _dot(xb, wu_ref[...])).astype(BF16)
    acc_scr[...] += _dot(act, wd_ref[...])

    @pl.when(k == pl.num_programs(0) - 1)
    def _():
        o_ref[...] = _layer_norm(DN_ALPHA * x_ref[...] + acc_scr[...], _row(g_ref, layer),
                                 _row(b_ref, layer))


def _ffn(x, w_gu_b, w_dn_b, ln2_g, ln2_b, layer):
    m = x.shape[0]
    n_chunks = D_FF // FF_CHUNK_W
    full = pl.BlockSpec((m, D_MODEL), lambda k: (0, 0))
    vec = pl.BlockSpec((DEPTH, D_MODEL), lambda k: (0, 0))
    return pl.pallas_call(
        functools.partial(_ffn_kernel, layer=layer),
        grid=(n_chunks,),
        in_specs=[full,
                  pl.BlockSpec((None, D_MODEL, FF_CHUNK_W), lambda k: (0, 0, k)),
                  pl.BlockSpec((None, D_MODEL, FF_CHUNK_W), lambda k: (0, 0, n_chunks + k)),
                  pl.BlockSpec((None, FF_CHUNK_W, D_MODEL), lambda k: (0, k, 0)),
                  vec, vec],
        out_specs=full,
        out_shape=jax.ShapeDtypeStruct((m, D_MODEL), F32),
        scratch_shapes=[pltpu.VMEM((m, D_MODEL), BF16),
                        pltpu.VMEM((m, D_MODEL), F32)],
        compiler_params=pltpu.CompilerParams(dimension_semantics=("arbitrary",),
                                             vmem_limit_bytes=VMEM_LIMIT_BYTES),
        name="ffn",
    )(x, w_gu_b, w_gu_b, w_dn_b, ln2_g, ln2_b)


def _rotary_tables(pos, rows):
    half = HEAD_DIM // 2
    inv_freq = ROPE_BASE ** (-np.arange(0, half, dtype=np.float64) / half)
    ang = np.asarray(pos, np.float64)[:, None] * inv_freq[None, :]
    cos, sin = np.cos(ang), np.sin(ang)
    cos2 = np.concatenate([cos, cos], axis=-1)
    sin2 = np.concatenate([-sin, sin], axis=-1)
    scale = HEAD_DIM ** -0.5
    return tuple(jnp.asarray(np.broadcast_to(t, (rows, HEAD_DIM)).astype(np.float32))
                 for t in (cos2, sin2, cos2 * scale, sin2 * scale))


def _decay_tables(chunk):
    log_g = np.log(1.0 - 2.0 ** (-5.0 - np.arange(HEADS, dtype=np.float64)))
    idx = np.arange(chunk, dtype=np.float64)
    diff = idx[:, None] - idx[None, :]
    dmask = np.where(diff[None] >= 0,
                     np.exp(np.maximum(diff, 0.0)[None] * log_g[:, None, None]), 0.0)
    q_decay = np.exp((idx + 1.0)[None, :] * log_g[:, None])
    k_decay = np.exp((chunk - 1.0 - idx)[None, :] * log_g[:, None])
    chunk_decay = np.exp(chunk * log_g)
    return tuple(t.astype(np.float32) for t in (dmask, q_decay, k_decay, chunk_decay))


def kernel(x_prompt, x_sample, state_conv, state_ret, w_in, conv_w, conv_b, conv_ln_g, conv_ln_b,
           ret_gn_g, w_out, ln1_g, ln1_b, w_gate_up, w_down, ln2_g, ln2_b):
    batch, seq, _ = x_prompt.shape
    nb = x_sample.shape[0]
    weights_f32 = (w_in, w_out, w_gate_up, w_down)
    weights_b = (w_in[0:1].astype(BF16), w_out[0:1].astype(BF16))
    conv_params = (conv_w, conv_b, conv_ln_g, conv_ln_b, ret_gn_g)
    conv_w_rows = jnp.broadcast_to(conv_w[:, :, None, :], (DEPTH, CONV_K, SUBLANES, CONV_CH))

    tab_p = _rotary_tables(np.arange(seq), seq)
    tab_s = _rotary_tables(PAST_LEN + np.arange(1), nb)

    dmask, q_dec, k_dec, c_dec = _decay_tables(CHUNK)
    bc = lambda a: jnp.asarray(np.broadcast_to(a, (HEADS, CHUNK, HEAD_DIM)))
    bc1 = lambda a: jnp.asarray(np.broadcast_to(a, (HEADS, HEAD_DIM, HEAD_DIM)))
    decay_p = (jnp.asarray(dmask), bc(q_dec[:, :, None]), bc(k_dec[:, :, None]),
               bc1(c_dec[:, None, None]))
    _, _, k_dec1, c_dec1 = _decay_tables(1)
    decay_s = (bc1(k_dec1[:, :, None]), bc1(c_dec1[:, None, None]))

    xp = x_prompt.reshape(batch * seq, D_MODEL)
    xs = x_sample.reshape(nb, D_MODEL)
    conv_p, ret_p = [], []
    bufs = None
    state_conv_t = jnp.transpose(state_conv, (0, 2, 1, 3))
    for layer in range(DEPTH):
        m_mix, cst, rst, ffn_b = _prompt_mix(
            xp, weights_b[0], weights_b[1], tab_p, (conv_w_rows,) + conv_params[1:], layer, decay_p,
            batch, seq, weights_f32[2:] if layer == 0 else ())
        w_in_b, w_out_b, w_gu_b, w_dn_b = weights_b + ffn_b
        conv_p.append(cst)
        ret_p.append(rst)
        xp, weights_b = _prompt_ffn(m_mix, xp, ln1_g, ln1_b, w_gu_b, w_dn_b, ln2_g, ln2_b,
                                    layer, weights_f32 if layer + 1 < DEPTH else ())

        u, q, k, v, sg = _mix_in(xs, w_in_b, tab_s)
        x1, cso, rso = _mix_core_sample(u, q, k, v, sg, xs, state_conv_t, state_ret,
                                        conv_params + (w_out_b, ln1_g, ln1_b), layer, decay_s, bufs)
        bufs = (cso, rso)
        xs = _ffn(x1, w_gu_b, w_dn_b, ln2_g, ln2_b, layer)

    return (xp.reshape(batch, seq, D_MODEL), xs.reshape(nb, 1, D_MODEL),
            jnp.stack(conv_p, axis=0), jnp.stack(ret_p, axis=0),
            jnp.transpose(bufs[0], (0, 2, 1, 3)), bufs[1])
```

```python
import functools

import jax
import jax.numpy as jnp
import numpy as np
from jax import lax
from jax.experimental import pallas as pl
from jax.experimental.pallas import tpu as pltpu

F32 = jnp.float32
BF16 = jnp.bfloat16

D_MODEL = 1024
DEPTH = 4
CONV_CH = 512
CONV_K = 31
HEADS = 4
HEAD_DIM = 128
RET_W = HEADS * HEAD_DIM
D_FF = 2816
IN_COLS = 2 * CONV_CH + 4 * RET_W
PAST_LEN = 16384
ROPE_BASE = 10000.0
DN_ALPHA = (2 * DEPTH) ** 0.25
LN_EPS = 1e-5
CHUNK = 256

SUBLANES = 8
VMEM_LIMIT_BYTES = 56 * 1024 * 1024

HIST = 32
CONV_OFF = HIST - (CONV_K - 1)
CONV_ROWS = 64
FF_CHUNK_W = 256
FF_CHUNKS = tuple((s, min(FF_CHUNK_W, D_FF - s)) for s in range(0, D_FF, FF_CHUNK_W))
CAST_STEPS = 2

PROMPT_TILE = 512
FFN_TILE = 512
SAMPLE_TILE = 16


def _dot(a, b):
    return jnp.dot(a, b, preferred_element_type=F32)


def _dot_nt(a, b):
    return lax.dot_general(a, b, (((1,), (1,)), ((), ())), preferred_element_type=F32)


def _layer_norm(y, g, b):
    mu = jnp.mean(y, axis=-1, keepdims=True)
    d = y - mu
    var = jnp.mean(d * d, axis=-1, keepdims=True)
    return d * lax.rsqrt(var + LN_EPS) * g + b


def _group_norm(o, g):
    mu = jnp.mean(o, axis=-1, keepdims=True)
    d = o - mu
    var = jnp.mean(d * d, axis=-1, keepdims=True)
    return d * lax.rsqrt(var + LN_EPS) * g


def _silu(x):
    return x * jax.nn.sigmoid(x)


def _row(ref, layer):
    return ref[layer:layer + 1, :]


def _rotary(seg, cos, sin_signed):
    return seg * cos + pltpu.roll(seg, HEAD_DIM // 2, 1) * sin_signed


def _shift_copies(win_ref, sh_ref, lo, hi):
    for r in range(1, SUBLANES):
        sh_ref[r - 1, lo:hi, :] = win_ref[r + lo:r + hi, :]


def _conv_block(win_ref, sh_ref, cw_ref, cbias, clg, clb, r0):
    acc = jnp.broadcast_to(cbias, (CONV_ROWS, CONV_CH))
    for j in range(CONV_K):
        a, r = divmod(CONV_OFF + j, SUBLANES)
        lo = r0 + a * SUBLANES
        src = win_ref[lo:lo + CONV_ROWS, :] if r == 0 else sh_ref[r - 1, lo:lo + CONV_ROWS, :]
        acc = acc + src * jnp.concatenate([cw_ref[j]] * (CONV_ROWS // SUBLANES), axis=0)
    return _silu(_layer_norm(acc, clg, clb)).astype(BF16)


def _swiglu_ffn(x1, wgu_ref, wdn_ref, a_scr):
    xb = x1.astype(BF16)
    for s, w in FF_CHUNKS:
        gate = _dot(xb, wgu_ref[:, s:s + w])
        up = _dot(xb, wgu_ref[:, D_FF + s:D_FF + s + w])
        a_scr[:, s:s + w] = (_silu(gate) * up).astype(BF16)
    return _dot(a_scr[...], wdn_ref[...])


def _cast_specs(weights, layer, n_steps, step):
    n_slabs = n_steps // CAST_STEPS
    ins, outs, shapes = [], [], []
    for w in weights:
        _, rows, cols = w.shape
        slab = rows // n_slabs
        ins.append(pl.BlockSpec((None, slab, cols), lambda *g: (layer, step(*g) // CAST_STEPS, 0)))
        outs.append(pl.BlockSpec((None, slab, cols), lambda *g: (0, step(*g) // CAST_STEPS, 0)))
        shapes.append(jax.ShapeDtypeStruct((1, rows, cols), BF16))
    return ins, outs, shapes


def _cast_slabs(src_refs, dst_refs):
    for src_ref, dst_ref in zip(src_refs, dst_refs):
        dst_ref[...] = src_ref[...].astype(BF16)


def _prompt_mix_kernel(x_ref, w_ref, cq_ref, sq_ref, ck_ref, sk_ref,
                       cw_ref, cb_ref, clg_ref, clb_ref, gn_ref, wout_ref,
                       dmask_ref, qdec_ref, kdec_ref, cdec_ref, *rest, tm, layer):
    n_cast = (len(rest) - 12) // 2
    m_ref, convst_ref, retst_ref = rest[n_cast:n_cast + 3]
    ubuf, shbuf, cr_ref, s_scr, q_scr, k_scr, kd_scr, v_scr, sg_scr = rest[2 * n_cast + 3:]
    l = pl.program_id(1)

    @pl.when(l == 0)
    def _():
        ubuf[0:HIST, :] = jnp.zeros((HIST, CONV_CH), F32)
        s_scr[...] = jnp.zeros_like(s_scr)

    _cast_slabs(rest[:n_cast], rest[n_cast + 3:2 * n_cast + 3])
    xb = x_ref[...].astype(BF16)
    n_chunks = tm // CHUNK

    half = tm // 2
    halo = HIST - SUBLANES

    def glu(p):
        rows = slice(p * half, (p + 1) * half)
        ca = _dot(xb[rows, :], w_ref[:, 0:CONV_CH])
        cb = _dot(xb[rows, :], w_ref[:, CONV_CH:2 * CONV_CH])
        ubuf[HIST + p * half:HIST + (p + 1) * half, :] = ca * jax.nn.sigmoid(cb)
        _shift_copies(ubuf, shbuf, p * half + (halo if p else 0), (p + 1) * half + halo)

    clg, clb, cbias = _row(clg_ref, layer), _row(clb_ref, layer), _row(cb_ref, layer)

    def conv_block(rb):
        r0 = rb * CONV_ROWS
        cr_ref[r0:r0 + CONV_ROWS, 0:CONV_CH] = _conv_block(ubuf, shbuf, cw_ref, cbias, clg, clb, r0)

    o = 2 * CONV_CH
    cq, sq, ck, sk = cq_ref[...], sq_ref[...], ck_ref[...], sk_ref[...]

    def proj_q():
        zq = _dot(xb, w_ref[:, o:o + RET_W])
        for h in range(HEADS):
            cols = slice(h * HEAD_DIM, (h + 1) * HEAD_DIM)
            q_scr[:, cols] = _rotary(zq[:, cols], cq, sq).astype(BF16)

    def proj_k():
        zk = _dot(xb, w_ref[:, o + RET_W:o + 2 * RET_W])
        for h in range(HEADS):
            cols = slice(h * HEAD_DIM, (h + 1) * HEAD_DIM)
            kr = _rotary(zk[:, cols], ck, sk)
            k_scr[:, cols] = kr.astype(BF16)
            for c in range(n_chunks):
                rows = slice(c * CHUNK, (c + 1) * CHUNK)
                kd_scr[rows, cols] = (kr[rows, :] * kdec_ref[h]).astype(BF16)

    def proj_v():
        v_scr[...] = _dot(xb, w_ref[:, o + 2 * RET_W:o + 3 * RET_W]).astype(BF16)

    def proj_gate():
        sg_scr[...] = _silu(_dot(xb, w_ref[:, o + 3 * RET_W:o + 4 * RET_W]))

    def retention(c, h):
        rows = slice(c * CHUNK, (c + 1) * CHUNK)
        cols = slice(h * HEAD_DIM, (h + 1) * HEAD_DIM)
        qc = q_scr[rows, cols]
        vc = v_scr[rows, cols]
        s = s_scr[h]
        sc = _dot_nt(qc, k_scr[rows, cols]) * dmask_ref[h]
        out = _dot(sc.astype(BF16), vc) + _dot(qc, s.astype(BF16)) * qdec_ref[h]
        s_scr[h] = s * cdec_ref[h] + _dot(kd_scr[rows, cols].T, vc)
        on = _group_norm(out, gn_ref[layer:layer + 1, cols])
        cr_ref[rows, CONV_CH + h * HEAD_DIM:CONV_CH + (h + 1) * HEAD_DIM] = (
            sg_scr[rows, cols] * on).astype(BF16)

    def out_proj(c):
        rows = slice(c * CHUNK, (c + 1) * CHUNK)
        m_ref[rows, :] = (_dot(cr_ref[rows, 0:CONV_CH], wout_ref[0:CONV_CH, :])
                          + _dot(cr_ref[rows, CONV_CH:D_MODEL], wout_ref[CONV_CH:D_MODEL, :]))

    glu(0)
    glu(1)
    stages = [proj_q, proj_k, proj_v, proj_gate]
    proj_stage = []
    for c in range(n_chunks):
        stages += [functools.partial(retention, c, h) for h in range(HEADS)]
        proj_stage.append(len(stages))
        stages.append(functools.partial(out_proj, c))
    n_blocks = tm // CONV_ROWS
    blocks_per_chunk = CHUNK // CONV_ROWS
    assert all((c + 1) * blocks_per_chunk <= proj_stage[c] for c in range(n_chunks))
    for p, stage in enumerate(stages):
        stage()
        if p < n_blocks:
            conv_block(p)

    ubuf[0:HIST, :] = ubuf[tm:tm + HIST, :]

    @pl.when(l == pl.num_programs(1) - 1)
    def _():
        convst_ref[...] = ubuf[CONV_OFF:HIST, :]
        retst_ref[...] = s_scr[...]


def _prompt_mix(x, w_in_b, w_out_b, tables, params, layer, decay, batch, seq, cast_weights):
    conv_w_rows, conv_b, conv_ln_g, conv_ln_b, ret_gn_g = params
    tm = PROMPT_TILE
    nl = seq // tm
    row = lambda b, l: (b * nl + l, 0)
    vec = lambda w: pl.BlockSpec((DEPTH, w), lambda b, l: (0, 0))
    tab_spec = pl.BlockSpec((tm, HEAD_DIM), lambda b, l: (l, 0))
    dec_spec = lambda rows, cols: pl.BlockSpec((HEADS, rows, cols), lambda b, l: (0, 0, 0))
    cast_in, cast_out, cast_shapes = _cast_specs(cast_weights, layer, batch * nl,
                                                 lambda b, l: b * nl + l)
    outs = pl.pallas_call(
        functools.partial(_prompt_mix_kernel, tm=tm, layer=layer),
        grid=(batch, nl),
        in_specs=[pl.BlockSpec((tm, D_MODEL), row),
                  pl.BlockSpec((None, D_MODEL, IN_COLS), lambda b, l: (0, 0, 0),
                               pipeline_mode=pl.Buffered(1)),
                  tab_spec, tab_spec, tab_spec, tab_spec,
                  pl.BlockSpec((None, CONV_K, SUBLANES, CONV_CH), lambda b, l: (layer, 0, 0, 0)),
                  vec(CONV_CH), vec(CONV_CH), vec(CONV_CH), vec(RET_W),
                  pl.BlockSpec((None, D_MODEL, D_MODEL), lambda b, l: (0, 0, 0),
                               pipeline_mode=pl.Buffered(1)),
                  dec_spec(CHUNK, CHUNK), dec_spec(CHUNK, HEAD_DIM), dec_spec(CHUNK, HEAD_DIM),
                  dec_spec(HEAD_DIM, HEAD_DIM)] + cast_in,
        out_specs=[pl.BlockSpec((tm, D_MODEL), row),
                   pl.BlockSpec((None, CONV_K - 1, CONV_CH), lambda b, l: (b, 0, 0)),
                   pl.BlockSpec((None, HEADS, HEAD_DIM, HEAD_DIM), lambda b, l: (b, 0, 0, 0))]
        + cast_out,
        out_shape=[jax.ShapeDtypeStruct((batch * seq, D_MODEL), F32),
                   jax.ShapeDtypeStruct((batch, CONV_K - 1, CONV_CH), F32),
                   jax.ShapeDtypeStruct((batch, HEADS, HEAD_DIM, HEAD_DIM), F32)] + cast_shapes,
        scratch_shapes=[pltpu.VMEM((HIST + tm, CONV_CH), F32),
                        pltpu.VMEM((SUBLANES - 1, HIST + tm - SUBLANES, CONV_CH), F32),
                        pltpu.VMEM((tm, D_MODEL), BF16),
                        pltpu.VMEM((HEADS, HEAD_DIM, HEAD_DIM), F32),
                        pltpu.VMEM((tm, RET_W), BF16),
                        pltpu.VMEM((tm, RET_W), BF16),
                        pltpu.VMEM((tm, RET_W), BF16),
                        pltpu.VMEM((tm, RET_W), BF16),
                        pltpu.VMEM((tm, RET_W), F32)],
        compiler_params=pltpu.CompilerParams(dimension_semantics=("arbitrary", "arbitrary"),
                                             vmem_limit_bytes=VMEM_LIMIT_BYTES),
        name="prompt_mix",
    )(x, w_in_b, *tables, conv_w_rows, conv_b, conv_ln_g, conv_ln_b, ret_gn_g, w_out_b, *decay,
      *cast_weights)
    return outs[0], outs[1], outs[2], tuple(outs[3:])


def _prompt_ffn_kernel(m_ref, x_ref, l1g_ref, l1b_ref, wgu_ref, wdn_ref, l2g_ref, l2b_ref,
                       *rest, tm, layer):
    n_cast = (len(rest) - 2) // 2
    o_ref, a_scr = rest[n_cast], rest[-1]
    _cast_slabs(rest[:n_cast], rest[n_cast + 1:-1])

    pm = tm // 2
    rows = (slice(0, pm), slice(pm, tm))
    l1g, l1b, l2g, l2b = (_row(r, layer) for r in (l1g_ref, l1b_ref, l2g_ref, l2b_ref))

    def norm1(p):
        return _layer_norm(DN_ALPHA * x_ref[rows[p], :] + m_ref[rows[p], :], l1g, l1b)

    def ff_chunk(p, xb, k):
        s, w = FF_CHUNKS[k]
        gate = _dot(xb, wgu_ref[:, s:s + w])
        up = _dot(xb, wgu_ref[:, D_FF + s:D_FF + s + w])
        a_scr[rows[p], s:s + w] = (_silu(gate) * up).astype(BF16)

    def down(p):
        return _dot(a_scr[rows[p], :], wdn_ref[...])

    def norm2(p, x1, f):
        o_ref[rows[p], :] = _layer_norm(DN_ALPHA * x1 + f, l2g, l2b)

    n_ff = len(FF_CHUNKS)
    x1_0 = norm1(0)
    xb0 = x1_0.astype(BF16)
    ff_chunk(0, xb0, 0)
    x1_1 = norm1(1)
    xb1 = x1_1.astype(BF16)
    for k in range(1, n_ff):
        ff_chunk(0, xb0, k)
    f0 = down(0)
    ff_chunk(1, xb1, 0)
    norm2(0, x1_0, f0)
    for k in range(1, n_ff):
        ff_chunk(1, xb1, k)
    norm2(1, x1_1, down(1))


def _prompt_ffn(m_mix, x, ln1_g, ln1_b, w_gu_b, w_dn_b, ln2_g, ln2_b, layer, next_weights):
    m = x.shape[0]
    tm = FFN_TILE
    n_steps = m // tm
    row = lambda i: (i, 0)
    fixed = lambda i: (0, 0, 0)
    once = pl.Buffered(1)
    vec = pl.BlockSpec((DEPTH, D_MODEL), lambda i: (0, 0))
    cast_in, cast_out, cast_shapes = _cast_specs(next_weights, layer + 1, n_steps, lambda i: i)
    outs = pl.pallas_call(
        functools.partial(_prompt_ffn_kernel, tm=tm, layer=layer),
        grid=(n_steps,),
        in_specs=[pl.BlockSpec((tm, D_MODEL), row),
                  pl.BlockSpec((tm, D_MODEL), row),
                  vec, vec,
                  pl.BlockSpec((None, D_MODEL, 2 * D_FF), fixed, pipeline_mode=once),
                  pl.BlockSpec((None, D_FF, D_MODEL), fixed, pipeline_mode=once),
                  vec, vec] + cast_in,
        out_specs=[pl.BlockSpec((tm, D_MODEL), row)] + cast_out,
        out_shape=[jax.ShapeDtypeStruct((m, D_MODEL), F32)] + cast_shapes,
        scratch_shapes=[pltpu.VMEM((tm, D_FF), BF16)],
        compiler_params=pltpu.CompilerParams(dimension_semantics=("arbitrary",),
                                             vmem_limit_bytes=VMEM_LIMIT_BYTES),
        name="prompt_ffn",
    )(m_mix, x, ln1_g, ln1_b, w_gu_b, w_dn_b, ln2_g, ln2_b, *next_weights)
    return outs[0], tuple(outs[1:])


def _mix_in_kernel(x_ref, w_ref, cq_ref, sq_ref, ck_ref, sk_ref,
                   u_ref, q_ref, k_ref, v_ref, sg_ref):
    xb = x_ref[...].astype(BF16)
    ca = _dot(xb, w_ref[:, 0:CONV_CH])
    cb = _dot(xb, w_ref[:, CONV_CH:2 * CONV_CH])
    u_ref[...] = ca * jax.nn.sigmoid(cb)
    o = 2 * CONV_CH
    zq = _dot(xb, w_ref[:, o:o + RET_W])
    zk = _dot(xb, w_ref[:, o + RET_W:o + 2 * RET_W])
    cq, sq, ck, sk = cq_ref[...], sq_ref[...], ck_ref[...], sk_ref[...]
    for h in range(HEADS):
        sl = slice(h * HEAD_DIM, (h + 1) * HEAD_DIM)
        q_ref[:, sl] = _rotary(zq[:, sl], cq, sq).astype(BF16)
        k_ref[:, sl] = _rotary(zk[:, sl], ck, sk)
    v_ref[...] = _dot(xb, w_ref[:, o + 2 * RET_W:o + 3 * RET_W]).astype(BF16)
    sg_ref[...] = _silu(_dot(xb, w_ref[:, o + 3 * RET_W:o + 4 * RET_W]))


def _mix_in(x, w_in_b, tables):
    m = x.shape[0]
    full = lambda w: pl.BlockSpec((m, w), lambda i: (0, 0))
    return pl.pallas_call(
        _mix_in_kernel,
        grid=(1,),
        in_specs=[full(D_MODEL),
                  pl.BlockSpec((None, D_MODEL, IN_COLS), lambda i: (0, 0, 0)),
                  full(HEAD_DIM), full(HEAD_DIM), full(HEAD_DIM), full(HEAD_DIM)],
        out_specs=[full(RET_W)] * 5,
        out_shape=[jax.ShapeDtypeStruct((m, CONV_CH), F32),
                   jax.ShapeDtypeStruct((m, RET_W), BF16),
                   jax.ShapeDtypeStruct((m, RET_W), F32),
                   jax.ShapeDtypeStruct((m, RET_W), BF16),
                   jax.ShapeDtypeStruct((m, RET_W), F32)],
        compiler_params=pltpu.CompilerParams(dimension_semantics=("arbitrary",),
                                             vmem_limit_bytes=VMEM_LIMIT_BYTES),
        name="mix_in",
    )(x, w_in_b, *tables)


def _mix_core_sample_kernel(*refs, tb, aliased, layer):
    if aliased:
        refs = refs[:18] + refs[20:]
    (u_ref, q_ref, k_ref, v_ref, sg_ref, x_ref, cst_ref, rst_ref,
     cw_ref, cb_ref, clg_ref, clb_ref, gn_ref, wout_ref, l1g_ref, l1b_ref,
     kdec_ref, cdec_ref,
     x1_ref, cso_ref, rso_ref,
     o_scr) = refs

    u = u_ref[...]
    conv = u * cw_ref[CONV_K - 1:CONV_K, :] + _row(cb_ref, layer)
    for j in range(CONV_K - 1):
        conv = conv + cst_ref[j] * cw_ref[j:j + 1, :]
        if j > 0:
            cso_ref[j - 1] = cst_ref[j]
    cso_ref[CONV_K - 2] = u
    c = _silu(_layer_norm(conv, _row(clg_ref, layer), _row(clb_ref, layer))).astype(BF16)

    wide = tb * HEAD_DIM
    in_block = (lax.broadcasted_iota(jnp.int32, (tb, wide), 1) // HEAD_DIM
                == lax.broadcasted_iota(jnp.int32, (tb, wide), 0))
    ones_bd = jnp.where(in_block, 1.0, 0.0).astype(BF16)
    contract_rows = (((0,), (0,)), ((), ()))
    for h in range(HEADS):
        cols = slice(h * HEAD_DIM, (h + 1) * HEAD_DIM)
        kd = (k_ref[:, cols] * kdec_ref[h, 0:tb, :]).astype(BF16)
        v_bd = jnp.where(in_block, jnp.concatenate([v_ref[:, cols].astype(F32)] * tb, axis=1),
                         0.0).astype(BF16)
        kv = lax.dot_general(kd, v_bd, contract_rows, preferred_element_type=F32)
        q_cols = lax.dot_general(q_ref[:, cols], ones_bd, contract_rows,
                                 preferred_element_type=F32)
        for b in range(tb):
            blk = slice(b * HEAD_DIM, (b + 1) * HEAD_DIM)
            s_new = rst_ref[b, h] * cdec_ref[h] + kv[:, blk]
            rso_ref[b, h] = s_new
            o_scr[b:b + 1, cols] = jnp.sum(q_cols[:, blk] * s_new, axis=0, keepdims=True)

    rs = []
    for h in range(HEADS):
        cols = slice(h * HEAD_DIM, (h + 1) * HEAD_DIM)
        on = _group_norm(o_scr[:, cols], gn_ref[layer:layer + 1, cols])
        rs.append((sg_ref[:, cols] * on).astype(BF16))
    cr = jnp.concatenate([c] + rs, axis=-1)
    m = _dot(cr, wout_ref[...])
    x1_ref[...] = _layer_norm(DN_ALPHA * x_ref[...] + m, _row(l1g_ref, layer), _row(l1b_ref, layer))


def _mix_core_sample(u, q, k, v, sg, x, state_conv, state_ret, params, layer, decay, bufs):
    conv_w, conv_b, conv_ln_g, conv_ln_b, ret_gn_g, w_out_b, ln1_g, ln1_b = params
    nb = x.shape[0]
    tb = SAMPLE_TILE
    row = lambda i: (i, 0)
    lay2 = lambda i: (layer, 0, 0)
    act = lambda w: pl.BlockSpec((tb, w), row)
    vec = lambda w: pl.BlockSpec((DEPTH, w), lambda i: (0, 0))
    dec_spec = pl.BlockSpec((HEADS, HEAD_DIM, HEAD_DIM), lambda i: (0, 0, 0))
    cst_spec = pl.BlockSpec((None, CONV_K - 1, tb, CONV_CH), lambda i: (layer, 0, i, 0))
    rst_spec = pl.BlockSpec((None, tb, HEADS, HEAD_DIM, HEAD_DIM), lambda i: (layer, i, 0, 0, 0))
    in_specs = [act(CONV_CH), act(RET_W), act(RET_W), act(RET_W), act(RET_W), act(D_MODEL),
                cst_spec, rst_spec,
                pl.BlockSpec((None, CONV_K, CONV_CH), lay2),
                vec(CONV_CH), vec(CONV_CH), vec(CONV_CH), vec(RET_W),
                pl.BlockSpec((None, D_MODEL, D_MODEL), lambda i: (0, 0, 0)),
                vec(D_MODEL), vec(D_MODEL),
                dec_spec, dec_spec]
    args = [u, q, k, v, sg, x, state_conv, state_ret,
            conv_w, conv_b, conv_ln_g, conv_ln_b, ret_gn_g, w_out_b, ln1_g, ln1_b, *decay]
    aliases = {}
    if bufs is not None:
        in_specs += [pl.BlockSpec(memory_space=pl.ANY), pl.BlockSpec(memory_space=pl.ANY)]
        args += list(bufs)
        aliases = {18: 1, 19: 2}
    return pl.pallas_call(
        functools.partial(_mix_core_sample_kernel, tb=tb, aliased=bufs is not None, layer=layer),
        grid=(nb // tb,),
        in_specs=in_specs,
        out_specs=[act(D_MODEL), cst_spec, rst_spec],
        out_shape=[jax.ShapeDtypeStruct((nb, D_MODEL), F32),
                   jax.ShapeDtypeStruct(state_conv.shape, F32),
                   jax.ShapeDtypeStruct(state_ret.shape, F32)],
        scratch_shapes=[pltpu.VMEM((tb, RET_W), F32)],
        input_output_aliases=aliases,
        compiler_params=pltpu.CompilerParams(dimension_semantics=("arbitrary",),
                                             vmem_limit_bytes=VMEM_LIMIT_BYTES),
        name="mix_core_sample",
    )(*args)


def _ffn_kernel(x_ref, wgu_ref, wdn_ref, g_ref, b_ref, o_ref, a_scr, *, layer):
    x = x_ref[...]
    f = _swiglu_ffn(x, wgu_ref, wdn_ref, a_scr)
    o_ref[...] = _layer_norm(DN_ALPHA * x + f, _row(g_ref, layer), _row(b_ref, layer))


def _ffn(x, w_gu_b, w_dn_b, ln2_g, ln2_b, layer):
    m = x.shape[0]
    once = pl.Buffered(1)
    vec = pl.BlockSpec((DEPTH, D_MODEL), lambda i: (0, 0))
    return pl.pallas_call(
        functools.partial(_ffn_kernel, layer=layer),
        grid=(1,),
        in_specs=[pl.BlockSpec((m, D_MODEL), lambda i: (0, 0)),
                  pl.BlockSpec((None, D_MODEL, 2 * D_FF), lambda i: (0, 0, 0), pipeline_mode=once),
                  pl.BlockSpec((None, D_FF, D_MODEL), lambda i: (0, 0, 0), pipeline_mode=once),
                  vec, vec],
        out_specs=pl.BlockSpec((m, D_MODEL), lambda i: (0, 0)),
        out_shape=jax.ShapeDtypeStruct((m, D_MODEL), F32),
        scratch_shapes=[pltpu.VMEM((m, D_FF), BF16)],
        compiler_params=pltpu.CompilerParams(dimension_semantics=("arbitrary",),
                                             vmem_limit_bytes=VMEM_LIMIT_BYTES),
        name="ffn",
    )(x, w_gu_b, w_dn_b, ln2_g, ln2_b)


def _rotary_tables(pos, rows):
    half = HEAD_DIM // 2
    inv_freq = ROPE_BASE ** (-np.arange(0, half, dtype=np.float64) / half)
    ang = np.asarray(pos, np.float64)[:, None] * inv_freq[None, :]
    cos, sin = np.cos(ang), np.sin(ang)
    cos2 = np.concatenate([cos, cos], axis=-1)
    sin2 = np.concatenate([-sin, sin], axis=-1)
    scale = HEAD_DIM ** -0.5
    return tuple(jnp.asarray(np.broadcast_to(t, (rows, HEAD_DIM)).astype(np.float32))
                 for t in (cos2, sin2, cos2 * scale, sin2 * scale))


def _decay_tables(chunk):
    log_g = np.log(1.0 - 2.0 ** (-5.0 - np.arange(HEADS, dtype=np.float64)))
    idx = np.arange(chunk, dtype=np.float64)
    diff = idx[:, None] - idx[None, :]
    dmask = np.where(diff[None] >= 0,
                     np.exp(np.maximum(diff, 0.0)[None] * log_g[:, None, None]), 0.0)
    q_decay = np.exp((idx + 1.0)[None, :] * log_g[:, None])
    k_decay = np.exp((chunk - 1.0 - idx)[None, :] * log_g[:, None])
    chunk_decay = np.exp(chunk * log_g)
    return tuple(t.astype(np.float32) for t in (dmask, q_decay, k_decay, chunk_decay))


def kernel(x_prompt, x_sample, state_conv, state_ret, w_in, conv_w, conv_b, conv_ln_g, conv_ln_b,
           ret_gn_g, w_out, ln1_g, ln1_b, w_gate_up, w_down, ln2_g, ln2_b):
    batch, seq, _ = x_prompt.shape
    nb = x_sample.shape[0]
    weights_f32 = (w_in, w_out, w_gate_up, w_down)
    weights_b = (w_in[0:1].astype(BF16), w_out[0:1].astype(BF16))
    conv_params = (conv_w, conv_b, conv_ln_g, conv_ln_b, ret_gn_g)
    conv_w_rows = jnp.broadcast_to(conv_w[:, :, None, :], (DEPTH, CONV_K, SUBLANES, CONV_CH))

    tab_p = _rotary_tables(np.arange(seq), seq)
    tab_s = _rotary_tables(PAST_LEN + np.arange(1), nb)

    dmask, q_dec, k_dec, c_dec = _decay_tables(CHUNK)
    bc = lambda a: jnp.asarray(np.broadcast_to(a, (HEADS, CHUNK, HEAD_DIM)))
    bc1 = lambda a: jnp.asarray(np.broadcast_to(a, (HEADS, HEAD_DIM, HEAD_DIM)))
    decay_p = (jnp.asarray(dmask), bc(q_dec[:, :, None]), bc(k_dec[:, :, None]),
               bc1(c_dec[:, None, None]))
    _, _, k_dec1, c_dec1 = _decay_tables(1)
    decay_s = (bc1(k_dec1[:, :, None]), bc1(c_dec1[:, None, None]))

    xp = x_prompt.reshape(batch * seq, D_MODEL)
    xs = x_sample.reshape(nb, D_MODEL)
    conv_p, ret_p = [], []
    bufs = None
    state_conv_t = jnp.transpose(state_conv, (0, 2, 1, 3))
    for layer in range(DEPTH):
        m_mix, cst, rst, ffn_b = _prompt_mix(
            xp, weights_b[0], weights_b[1], tab_p, (conv_w_rows,) + conv_params[1:], layer, decay_p,
            batch, seq, weights_f32[2:] if layer == 0 else ())
        w_in_b, w_out_b, w_gu_b, w_dn_b = weights_b + ffn_b
        conv_p.append(cst)
        ret_p.append(rst)
        xp, weights_b = _prompt_ffn(m_mix, xp, ln1_g, ln1_b, w_gu_b, w_dn_b, ln2_g, ln2_b,
                                    layer, weights_f32 if layer + 1 < DEPTH else ())

        u, q, k, v, sg = _mix_in(xs, w_in_b, tab_s)
        x1, cso, rso = _mix_core_sample(u, q, k, v, sg, xs, state_conv_t, state_ret,
                                        conv_params + (w_out_b, ln1_g, ln1_b), layer, decay_s, bufs)
        bufs = (cso, rso)
        xs = _ffn(x1, w_gu_b, w_dn_b, ln2_g, ln2_b, layer)

    return (xp.reshape(batch, seq, D_MODEL), xs.reshape(nb, 1, D_MODEL),
            jnp.stack(conv_p, axis=0), jnp.stack(ret_p, axis=0),
            jnp.transpose(bufs[0], (0, 2, 1, 3)), bufs[1])
```

```python
import functools

import jax
import jax.numpy as jnp
import numpy as np
from jax import lax
from jax.experimental import pallas as pl
from jax.experimental.pallas import tpu as pltpu

F32 = jnp.float32
BF16 = jnp.bfloat16

D_MODEL = 1024
DEPTH = 4
CONV_CH = 512
CONV_K = 31
HEADS = 4
HEAD_DIM = 128
RET_W = HEADS * HEAD_DIM
D_FF = 2816
IN_COLS = 2 * CONV_CH + 4 * RET_W
PAST_LEN = 16384
ROPE_BASE = 10000.0
DN_ALPHA = (2 * DEPTH) ** 0.25
LN_EPS = 1e-5
CHUNK = 256

SUBLANES = 8
VMEM_LIMIT_BYTES = 56 * 1024 * 1024

HIST = 32
CONV_OFF = HIST - (CONV_K - 1)
CONV_ROWS = 64
FF_CHUNK_W = 256
FF_CHUNKS = tuple((s, min(FF_CHUNK_W, D_FF - s)) for s in range(0, D_FF, FF_CHUNK_W))
CAST_STEPS = 2

PROMPT_TILE = 512
FFN_TILE = 512
SAMPLE_TILE = 16


def _dot(a, b):
    return jnp.dot(a, b, preferred_element_type=F32)


def _dot_nt(a, b):
    return lax.dot_general(a, b, (((1,), (1,)), ((), ())), preferred_element_type=F32)


def _layer_norm(y, g, b):
    mu = jnp.mean(y, axis=-1, keepdims=True)
    d = y - mu
    var = jnp.mean(d * d, axis=-1, keepdims=True)
    return d * lax.rsqrt(var + LN_EPS) * g + b


def _group_norm(o, g):
    mu = jnp.mean(o, axis=-1, keepdims=True)
    d = o - mu
    var = jnp.mean(d * d, axis=-1, keepdims=True)
    return d * lax.rsqrt(var + LN_EPS) * g


def _silu(x):
    return x * jax.nn.sigmoid(x)


def _row(ref, layer):
    return ref[layer:layer + 1, :]


def _rotary(seg, cos, sin_signed):
    return seg * cos + pltpu.roll(seg, HEAD_DIM // 2, 1) * sin_signed


def _shift_copies(win_ref, sh_ref, lo, hi):
    for r in range(1, SUBLANES):
        sh_ref[r - 1, lo:hi, :] = win_ref[r + lo:r + hi, :]


def _conv_block(win_ref, sh_ref, cw_ref, cbias, clg, clb, r0):
    acc = jnp.broadcast_to(cbias, (CONV_ROWS, CONV_CH))
    for j in range(CONV_K):
        a, r = divmod(CONV_OFF + j, SUBLANES)
        lo = r0 + a * SUBLANES
        src = win_ref[lo:lo + CONV_ROWS, :] if r == 0 else sh_ref[r - 1, lo:lo + CONV_ROWS, :]
        acc = acc + src * jnp.concatenate([cw_ref[j]] * (CONV_ROWS // SUBLANES), axis=0)
    return _silu(_layer_norm(acc, clg, clb)).astype(BF16)


def _swiglu_ffn(x1, wgu_ref, wdn_ref, a_scr):
    xb = x1.astype(BF16)
    for s, w in FF_CHUNKS:
        gate = _dot(xb, wgu_ref[:, s:s + w])
        up = _dot(xb, wgu_ref[:, D_FF + s:D_FF + s + w])
        a_scr[:, s:s + w] = (_silu(gate) * up).astype(BF16)
    return _dot(a_scr[...], wdn_ref[...])


def _cast_specs(weights, layer, n_steps, step):
    n_slabs = n_steps // CAST_STEPS
    ins, outs, shapes = [], [], []
    for w in weights:
        _, rows, cols = w.shape
        slab = rows // n_slabs
        ins.append(pl.BlockSpec((None, slab, cols), lambda *g: (layer, step(*g) // CAST_STEPS, 0)))
        outs.append(pl.BlockSpec((None, slab, cols), lambda *g: (0, step(*g) // CAST_STEPS, 0)))
        shapes.append(jax.ShapeDtypeStruct((1, rows, cols), BF16))
    return ins, outs, shapes


def _cast_slabs(src_refs, dst_refs):
    for src_ref, dst_ref in zip(src_refs, dst_refs):
        dst_ref[...] = src_ref[...].astype(BF16)


def _prompt_mix_kernel(x_ref, w_ref, cq_ref, sq_ref, ck_ref, sk_ref,
                       cw_ref, cb_ref, clg_ref, clb_ref, gn_ref, wout_ref,
                       dmask_ref, qdec_ref, kdec_ref, cdec_ref, *rest, tm, layer):
    n_cast = (len(rest) - 12) // 2
    m_ref, convst_ref, retst_ref = rest[n_cast:n_cast + 3]
    ubuf, shbuf, cr_ref, s_scr, q_scr, k_scr, kd_scr, v_scr, sg_scr = rest[2 * n_cast + 3:]
    l = pl.program_id(1)

    @pl.when(l == 0)
    def _():
        ubuf[0:HIST, :] = jnp.zeros((HIST, CONV_CH), F32)
        s_scr[...] = jnp.zeros_like(s_scr)

    _cast_slabs(rest[:n_cast], rest[n_cast + 3:2 * n_cast + 3])
    xb = x_ref[...].astype(BF16)
    n_chunks = tm // CHUNK

    half = tm // 2
    halo = HIST - SUBLANES

    def glu(p):
        rows = slice(p * half, (p + 1) * half)
        ca = _dot(xb[rows, :], w_ref[:, 0:CONV_CH])
        cb = _dot(xb[rows, :], w_ref[:, CONV_CH:2 * CONV_CH])
        ubuf[HIST + p * half:HIST + (p + 1) * half, :] = ca * jax.nn.sigmoid(cb)
        _shift_copies(ubuf, shbuf, p * half + (halo if p else 0), (p + 1) * half + halo)

    clg, clb, cbias = _row(clg_ref, layer), _row(clb_ref, layer), _row(cb_ref, layer)

    def conv_block(rb):
        r0 = rb * CONV_ROWS
        cr_ref[r0:r0 + CONV_ROWS, 0:CONV_CH] = _conv_block(ubuf, shbuf, cw_ref, cbias, clg, clb, r0)

    o = 2 * CONV_CH
    cq, sq, ck, sk = cq_ref[...], sq_ref[...], ck_ref[...], sk_ref[...]

    def proj_q():
        zq = _dot(xb, w_ref[:, o:o + RET_W])
        for h in range(HEADS):
            cols = slice(h * HEAD_DIM, (h + 1) * HEAD_DIM)
            q_scr[:, cols] = _rotary(zq[:, cols], cq, sq).astype(BF16)

    def proj_k():
        zk = _dot(xb, w_ref[:, o + RET_W:o + 2 * RET_W])
        for h in range(HEADS):
            cols = slice(h * HEAD_DIM, (h + 1) * HEAD_DIM)
            kr = _rotary(zk[:, cols], ck, sk)
            k_scr[:, cols] = kr.astype(BF16)
            for c in range(n_chunks):
                rows = slice(c * CHUNK, (c + 1) * CHUNK)
                kd_scr[rows, cols] = (kr[rows, :] * kdec_ref[h]).astype(BF16)

    def proj_v():
        v_scr[...] = _dot(xb, w_ref[:, o + 2 * RET_W:o + 3 * RET_W]).astype(BF16)

    def proj_gate():
        sg_scr[...] = _silu(_dot(xb, w_ref[:, o + 3 * RET_W:o + 4 * RET_W]))

    def retention(c, h):
        rows = slice(c * CHUNK, (c + 1) * CHUNK)
        cols = slice(h * HEAD_DIM, (h + 1) * HEAD_DIM)
        qc = q_scr[rows, cols]
        vc = v_scr[rows, cols]
        s = s_scr[h]
        sc = _dot_nt(qc, k_scr[rows, cols]) * dmask_ref[h]
        out = _dot(sc.astype(BF16), vc) + _dot(qc, s.astype(BF16)) * qdec_ref[h]
        s_scr[h] = s * cdec_ref[h] + _dot(kd_scr[rows, cols].T, vc)
        on = _group_norm(out, gn_ref[layer:layer + 1, cols])
        cr_ref[rows, CONV_CH + h * HEAD_DIM:CONV_CH + (h + 1) * HEAD_DIM] = (
            sg_scr[rows, cols] * on).astype(BF16)

    def out_proj(c):
        rows = slice(c * CHUNK, (c + 1) * CHUNK)
        mid = CONV_CH + RET_W // 2
        m_ref[rows, :] = (_dot(cr_ref[rows, 0:CONV_CH], wout_ref[0:CONV_CH, :])
                          + _dot(cr_ref[rows, CONV_CH:mid], wout_ref[CONV_CH:mid, :])
                          + _dot(cr_ref[rows, mid:D_MODEL], wout_ref[mid:D_MODEL, :]))

    glu(0)
    glu(1)
    stages = [proj_q, proj_k, proj_v, proj_gate]
    proj_stage = []
    for c in range(n_chunks):
        stages += [functools.partial(retention, c, h) for h in range(HEADS)]
        proj_stage.append(len(stages))
        stages.append(functools.partial(out_proj, c))
    n_blocks = tm // CONV_ROWS
    blocks_per_chunk = CHUNK // CONV_ROWS
    assert all((c + 1) * blocks_per_chunk <= proj_stage[c] for c in range(n_chunks))
    for p, stage in enumerate(stages):
        stage()
        if p < n_blocks:
            conv_block(p)

    ubuf[0:HIST, :] = ubuf[tm:tm + HIST, :]

    @pl.when(l == pl.num_programs(1) - 1)
    def _():
        convst_ref[...] = ubuf[CONV_OFF:HIST, :]
        retst_ref[...] = s_scr[...]


def _prompt_mix(x, w_in_b, w_out_b, tables, params, layer, decay, batch, seq, cast_weights):
    conv_w_rows, conv_b, conv_ln_g, conv_ln_b, ret_gn_g = params
    tm = PROMPT_TILE
    nl = seq // tm
    row = lambda b, l: (b * nl + l, 0)
    vec = lambda w: pl.BlockSpec((DEPTH, w), lambda b, l: (0, 0))
    tab_spec = pl.BlockSpec((tm, HEAD_DIM), lambda b, l: (l, 0))
    dec_spec = lambda rows, cols: pl.BlockSpec((HEADS, rows, cols), lambda b, l: (0, 0, 0))
    cast_in, cast_out, cast_shapes = _cast_specs(cast_weights, layer, batch * nl,
                                                 lambda b, l: b * nl + l)
    outs = pl.pallas_call(
        functools.partial(_prompt_mix_kernel, tm=tm, layer=layer),
        grid=(batch, nl),
        in_specs=[pl.BlockSpec((tm, D_MODEL), row),
                  pl.BlockSpec((None, D_MODEL, IN_COLS), lambda b, l: (0, 0, 0),
                               pipeline_mode=pl.Buffered(1)),
                  tab_spec, tab_spec, tab_spec, tab_spec,
                  pl.BlockSpec((None, CONV_K, SUBLANES, CONV_CH), lambda b, l: (layer, 0, 0, 0)),
                  vec(CONV_CH), vec(CONV_CH), vec(CONV_CH), vec(RET_W),
                  pl.BlockSpec((None, D_MODEL, D_MODEL), lambda b, l: (0, 0, 0),
                               pipeline_mode=pl.Buffered(1)),
                  dec_spec(CHUNK, CHUNK), dec_spec(CHUNK, HEAD_DIM), dec_spec(CHUNK, HEAD_DIM),
                  dec_spec(HEAD_DIM, HEAD_DIM)] + cast_in,
        out_specs=[pl.BlockSpec((tm, D_MODEL), row),
                   pl.BlockSpec((None, CONV_K - 1, CONV_CH), lambda b, l: (b, 0, 0)),
                   pl.BlockSpec((None, HEADS, HEAD_DIM, HEAD_DIM), lambda b, l: (b, 0, 0, 0))]
        + cast_out,
        out_shape=[jax.ShapeDtypeStruct((batch * seq, D_MODEL), F32),
                   jax.ShapeDtypeStruct((batch, CONV_K - 1, CONV_CH), F32),
                   jax.ShapeDtypeStruct((batch, HEADS, HEAD_DIM, HEAD_DIM), F32)] + cast_shapes,
        scratch_shapes=[pltpu.VMEM((HIST + tm, CONV_CH), F32),
                        pltpu.VMEM((SUBLANES - 1, HIST + tm - SUBLANES, CONV_CH), F32),
                        pltpu.VMEM((tm, D_MODEL), BF16),
                        pltpu.VMEM((HEADS, HEAD_DIM, HEAD_DIM), F32),
                        pltpu.VMEM((tm, RET_W), BF16),
                        pltpu.VMEM((tm, RET_W), BF16),
                        pltpu.VMEM((tm, RET_W), BF16),
                        pltpu.VMEM((tm, RET_W), BF16),
                        pltpu.VMEM((tm, RET_W), F32)],
        compiler_params=pltpu.CompilerParams(dimension_semantics=("arbitrary", "arbitrary"),
                                             vmem_limit_bytes=VMEM_LIMIT_BYTES),
        name="prompt_mix",
    )(x, w_in_b, *tables, conv_w_rows, conv_b, conv_ln_g, conv_ln_b, ret_gn_g, w_out_b, *decay,
      *cast_weights)
    return outs[0], outs[1], outs[2], tuple(outs[3:])


def _prompt_ffn_kernel(m_ref, x_ref, l1g_ref, l1b_ref, wgu_ref, wdn_ref, l2g_ref, l2b_ref,
                       *rest, tm, layer):
    n_cast = (len(rest) - 2) // 2
    o_ref, a_scr = rest[n_cast], rest[-1]
    _cast_slabs(rest[:n_cast], rest[n_cast + 1:-1])

    pm = tm // 2
    rows = (slice(0, pm), slice(pm, tm))
    l1g, l1b, l2g, l2b = (_row(r, layer) for r in (l1g_ref, l1b_ref, l2g_ref, l2b_ref))

    def norm1(p):
        return _layer_norm(DN_ALPHA * x_ref[rows[p], :] + m_ref[rows[p], :], l1g, l1b)

    def ff_chunk(p, xb, k):
        s, w = FF_CHUNKS[k]
        gate = _dot(xb, wgu_ref[:, s:s + w])
        up = _dot(xb, wgu_ref[:, D_FF + s:D_FF + s + w])
        a_scr[rows[p], s:s + w] = (_silu(gate) * up).astype(BF16)

    def down(p):
        return _dot(a_scr[rows[p], :], wdn_ref[...])

    def norm2(p, x1, f):
        o_ref[rows[p], :] = _layer_norm(DN_ALPHA * x1 + f, l2g, l2b)

    n_ff = len(FF_CHUNKS)
    x1_0 = norm1(0)
    xb0 = x1_0.astype(BF16)
    ff_chunk(0, xb0, 0)
    x1_1 = norm1(1)
    xb1 = x1_1.astype(BF16)
    for k in range(1, n_ff):
        ff_chunk(0, xb0, k)
    f0 = down(0)
    ff_chunk(1, xb1, 0)
    norm2(0, x1_0, f0)
    for k in range(1, n_ff):
        ff_chunk(1, xb1, k)
    norm2(1, x1_1, down(1))


def _prompt_ffn(m_mix, x, ln1_g, ln1_b, w_gu_b, w_dn_b, ln2_g, ln2_b, layer, next_weights):
    m = x.shape[0]
    tm = FFN_TILE
    n_steps = m // tm
    row = lambda i: (i, 0)
    fixed = lambda i: (0, 0, 0)
    once = pl.Buffered(1)
    vec = pl.BlockSpec((DEPTH, D_MODEL), lambda i: (0, 0))
    cast_in, cast_out, cast_shapes = _cast_specs(next_weights, layer + 1, n_steps, lambda i: i)
    outs = pl.pallas_call(
        functools.partial(_prompt_ffn_kernel, tm=tm, layer=layer),
        grid=(n_steps,),
        in_specs=[pl.BlockSpec((tm, D_MODEL), row),
                  pl.BlockSpec((tm, D_MODEL), row),
                  vec, vec,
                  pl.BlockSpec((None, D_MODEL, 2 * D_FF), fixed, pipeline_mode=once),
                  pl.BlockSpec((None, D_FF, D_MODEL), fixed, pipeline_mode=once),
                  vec, vec] + cast_in,
        out_specs=[pl.BlockSpec((tm, D_MODEL), row)] + cast_out,
        out_shape=[jax.ShapeDtypeStruct((m, D_MODEL), F32)] + cast_shapes,
        scratch_shapes=[pltpu.VMEM((tm, D_FF), BF16)],
        compiler_params=pltpu.CompilerParams(dimension_semantics=("arbitrary",),
                                             vmem_limit_bytes=VMEM_LIMIT_BYTES),
        name="prompt_ffn",
    )(m_mix, x, ln1_g, ln1_b, w_gu_b, w_dn_b, ln2_g, ln2_b, *next_weights)
    return outs[0], tuple(outs[1:])


def _mix_in_kernel(x_ref, w_ref, cq_ref, sq_ref, ck_ref, sk_ref,
                   u_ref, q_ref, k_ref, v_ref, sg_ref):
    xb = x_ref[...].astype(BF16)
    ca = _dot(xb, w_ref[:, 0:CONV_CH])
    cb = _dot(xb, w_ref[:, CONV_CH:2 * CONV_CH])
    u_ref[...] = ca * jax.nn.sigmoid(cb)
    o = 2 * CONV_CH
    zq = _dot(xb, w_ref[:, o:o + RET_W])
    zk = _dot(xb, w_ref[:, o + RET_W:o + 2 * RET_W])
    cq, sq, ck, sk = cq_ref[...], sq_ref[...], ck_ref[...], sk_ref[...]
    for h in range(HEADS):
        sl = slice(h * HEAD_DIM, (h + 1) * HEAD_DIM)
        q_ref[:, sl] = _rotary(zq[:, sl], cq, sq).astype(BF16)
        k_ref[:, sl] = _rotary(zk[:, sl], ck, sk)
    v_ref[...] = _dot(xb, w_ref[:, o + 2 * RET_W:o + 3 * RET_W]).astype(BF16)
    sg_ref[...] = _silu(_dot(xb, w_ref[:, o + 3 * RET_W:o + 4 * RET_W]))


def _mix_in(x, w_in_b, tables):
    m = x.shape[0]
    full = lambda w: pl.BlockSpec((m, w), lambda i: (0, 0))
    return pl.pallas_call(
        _mix_in_kernel,
        grid=(1,),
        in_specs=[full(D_MODEL),
                  pl.BlockSpec((None, D_MODEL, IN_COLS), lambda i: (0, 0, 0)),
                  full(HEAD_DIM), full(HEAD_DIM), full(HEAD_DIM), full(HEAD_DIM)],
        out_specs=[full(RET_W)] * 5,
        out_shape=[jax.ShapeDtypeStruct((m, CONV_CH), F32),
                   jax.ShapeDtypeStruct((m, RET_W), BF16),
                   jax.ShapeDtypeStruct((m, RET_W), F32),
                   jax.ShapeDtypeStruct((m, RET_W), BF16),
                   jax.ShapeDtypeStruct((m, RET_W), F32)],
        compiler_params=pltpu.CompilerParams(dimension_semantics=("arbitrary",),
                                             vmem_limit_bytes=VMEM_LIMIT_BYTES),
        name="mix_in",
    )(x, w_in_b, *tables)


def _mix_core_sample_kernel(*refs, tb, aliased, layer):
    if aliased:
        refs = refs[:18] + refs[20:]
    (u_ref, q_ref, k_ref, v_ref, sg_ref, x_ref, cst_ref, rst_ref,
     cw_ref, cb_ref, clg_ref, clb_ref, gn_ref, wout_ref, l1g_ref, l1b_ref,
     kdec_ref, cdec_ref,
     x1_ref, cso_ref, rso_ref,
     o_scr) = refs

    u = u_ref[...]
    conv = u * cw_ref[CONV_K - 1:CONV_K, :] + _row(cb_ref, layer)
    for j in range(CONV_K - 1):
        conv = conv + cst_ref[j] * cw_ref[j:j + 1, :]
        if j > 0:
            cso_ref[j - 1] = cst_ref[j]
    cso_ref[CONV_K - 2] = u
    c = _silu(_layer_norm(conv, _row(clg_ref, layer), _row(clb_ref, layer))).astype(BF16)

    wide = tb * HEAD_DIM
    in_block = (lax.broadcasted_iota(jnp.int32, (tb, wide), 1) // HEAD_DIM
                == lax.broadcasted_iota(jnp.int32, (tb, wide), 0))
    ones_bd = jnp.where(in_block, 1.0, 0.0).astype(BF16)
    contract_rows = (((0,), (0,)), ((), ()))
    for h in range(HEADS):
        cols = slice(h * HEAD_DIM, (h + 1) * HEAD_DIM)
        kd = (k_ref[:, cols] * kdec_ref[h, 0:tb, :]).astype(BF16)
        v_bd = jnp.where(in_block, jnp.concatenate([v_ref[:, cols].astype(F32)] * tb, axis=1),
                         0.0).astype(BF16)
        kv = lax.dot_general(kd, v_bd, contract_rows, preferred_element_type=F32)
        q_cols = lax.dot_general(q_ref[:, cols], ones_bd, contract_rows,
                                 preferred_element_type=F32)
        for b in range(tb):
            blk = slice(b * HEAD_DIM, (b + 1) * HEAD_DIM)
            s_new = rst_ref[b, h] * cdec_ref[h] + kv[:, blk]
            rso_ref[b, h] = s_new
            o_scr[b:b + 1, cols] = jnp.sum(q_cols[:, blk] * s_new, axis=0, keepdims=True)

    rs = []
    for h in range(HEADS):
        cols = slice(h * HEAD_DIM, (h + 1) * HEAD_DIM)
        on = _group_norm(o_scr[:, cols], gn_ref[layer:layer + 1, cols])
        rs.append((sg_ref[:, cols] * on).astype(BF16))
    cr = jnp.concatenate([c] + rs, axis=-1)
    m = _dot(cr, wout_ref[...])
    x1_ref[...] = _layer_norm(DN_ALPHA * x_ref[...] + m, _row(l1g_ref, layer), _row(l1b_ref, layer))


def _mix_core_sample(u, q, k, v, sg, x, state_conv, state_ret, params, layer, decay, bufs):
    conv_w, conv_b, conv_ln_g, conv_ln_b, ret_gn_g, w_out_b, ln1_g, ln1_b = params
    nb = x.shape[0]
    tb = SAMPLE_TILE
    row = lambda i: (i, 0)
    lay2 = lambda i: (layer, 0, 0)
    act = lambda w: pl.BlockSpec((tb, w), row)
    vec = lambda w: pl.BlockSpec((DEPTH, w), lambda i: (0, 0))
    dec_spec = pl.BlockSpec((HEADS, HEAD_DIM, HEAD_DIM), lambda i: (0, 0, 0))
    cst_spec = pl.BlockSpec((None, CONV_K - 1, tb, CONV_CH), lambda i: (layer, 0, i, 0))
    rst_spec = pl.BlockSpec((None, tb, HEADS, HEAD_DIM, HEAD_DIM), lambda i: (layer, i, 0, 0, 0))
    in_specs = [act(CONV_CH), act(RET_W), act(RET_W), act(RET_W), act(RET_W), act(D_MODEL),
                cst_spec, rst_spec,
                pl.BlockSpec((None, CONV_K, CONV_CH), lay2),
                vec(CONV_CH), vec(CONV_CH), vec(CONV_CH), vec(RET_W),
                pl.BlockSpec((None, D_MODEL, D_MODEL), lambda i: (0, 0, 0)),
                vec(D_MODEL), vec(D_MODEL),
                dec_spec, dec_spec]
    args = [u, q, k, v, sg, x, state_conv, state_ret,
            conv_w, conv_b, conv_ln_g, conv_ln_b, ret_gn_g, w_out_b, ln1_g, ln1_b, *decay]
    aliases = {}
    if bufs is not None:
        in_specs += [pl.BlockSpec(memory_space=pl.ANY), pl.BlockSpec(memory_space=pl.ANY)]
        args += list(bufs)
        aliases = {18: 1, 19: 2}
    return pl.pallas_call(
        functools.partial(_mix_core_sample_kernel, tb=tb, aliased=bufs is not None, layer=layer),
        grid=(nb // tb,),
        in_specs=in_specs,
        out_specs=[act(D_MODEL), cst_spec, rst_spec],
        out_shape=[jax.ShapeDtypeStruct((nb, D_MODEL), F32),
                   jax.ShapeDtypeStruct(state_conv.shape, F32),
                   jax.ShapeDtypeStruct(state_ret.shape, F32)],
        scratch_shapes=[pltpu.VMEM((tb, RET_W), F32)],
        input_output_aliases=aliases,
        compiler_params=pltpu.CompilerParams(dimension_semantics=("arbitrary",),
                                             vmem_limit_bytes=VMEM_LIMIT_BYTES),
        name="mix_core_sample",
    )(*args)


def _ffn_kernel(x_ref, wgu_ref, wdn_ref, g_ref, b_ref, o_ref, a_scr, *, layer):
    x = x_ref[...]
    f = _swiglu_ffn(x, wgu_ref, wdn_ref, a_scr)
    o_ref[...] = _layer_norm(DN_ALPHA * x + f, _row(g_ref, layer), _row(b_ref, layer))


def _ffn(x, w_gu_b, w_dn_b, ln2_g, ln2_b, layer):
    m = x.shape[0]
    once = pl.Buffered(1)
    vec = pl.BlockSpec((DEPTH, D_MODEL), lambda i: (0, 0))
    return pl.pallas_call(
        functools.partial(_ffn_kernel, layer=layer),
        grid=(1,),
        in_specs=[pl.BlockSpec((m, D_MODEL), lambda i: (0, 0)),
                  pl.BlockSpec((None, D_MODEL, 2 * D_FF), lambda i: (0, 0, 0), pipeline_mode=once),
                  pl.BlockSpec((None, D_FF, D_MODEL), lambda i: (0, 0, 0), pipeline_mode=once),
                  vec, vec],
        out_specs=pl.BlockSpec((m, D_MODEL), lambda i: (0, 0)),
        out_shape=jax.ShapeDtypeStruct((m, D_MODEL), F32),
        scratch_shapes=[pltpu.VMEM((m, D_FF), BF16)],
        compiler_params=pltpu.CompilerParams(dimension_semantics=("arbitrary",),
                                             vmem_limit_bytes=VMEM_LIMIT_BYTES),
        name="ffn",
    )(x, w_gu_b, w_dn_b, ln2_g, ln2_b)


def _rotary_tables(pos, rows):
    half = HEAD_DIM // 2
    inv_freq = ROPE_BASE ** (-np.arange(0, half, dtype=np.float64) / half)
    ang = np.asarray(pos, np.float64)[:, None] * inv_freq[None, :]
    cos, sin = np.cos(ang), np.sin(ang)
    cos2 = np.concatenate([cos, cos], axis=-1)
    sin2 = np.concatenate([-sin, sin], axis=-1)
    scale = HEAD_DIM ** -0.5
    return tuple(jnp.asarray(np.broadcast_to(t, (rows, HEAD_DIM)).astype(np.float32))
                 for t in (cos2, sin2, cos2 * scale, sin2 * scale))


def _decay_tables(chunk):
    log_g = np.log(1.0 - 2.0 ** (-5.0 - np.arange(HEADS, dtype=np.float64)))
    idx = np.arange(chunk, dtype=np.float64)
    diff = idx[:, None] - idx[None, :]
    dmask = np.where(diff[None] >= 0,
                     np.exp(np.maximum(diff, 0.0)[None] * log_g[:, None, None]), 0.0)
    q_decay = np.exp((idx + 1.0)[None, :] * log_g[:, None])
    k_decay = np.exp((chunk - 1.0 - idx)[None, :] * log_g[:, None])
    chunk_decay = np.exp(chunk * log_g)
    return tuple(t.astype(np.float32) for t in (dmask, q_decay, k_decay, chunk_decay))


def kernel(x_prompt, x_sample, state_conv, state_ret, w_in, conv_w, conv_b, conv_ln_g, conv_ln_b,
           ret_gn_g, w_out, ln1_g, ln1_b, w_gate_up, w_down, ln2_g, ln2_b):
    batch, seq, _ = x_prompt.shape
    nb = x_sample.shape[0]
    weights_f32 = (w_in, w_out, w_gate_up, w_down)
    weights_b = (w_in[0:1].astype(BF16), w_out[0:1].astype(BF16))
    conv_params = (conv_w, conv_b, conv_ln_g, conv_ln_b, ret_gn_g)
    conv_w_rows = jnp.broadcast_to(conv_w[:, :, None, :], (DEPTH, CONV_K, SUBLANES, CONV_CH))

    tab_p = _rotary_tables(np.arange(seq), seq)
    tab_s = _rotary_tables(PAST_LEN + np.arange(1), nb)

    dmask, q_dec, k_dec, c_dec = _decay_tables(CHUNK)
    bc = lambda a: jnp.asarray(np.broadcast_to(a, (HEADS, CHUNK, HEAD_DIM)))
    bc1 = lambda a: jnp.asarray(np.broadcast_to(a, (HEADS, HEAD_DIM, HEAD_DIM)))
    decay_p = (jnp.asarray(dmask), bc(q_dec[:, :, None]), bc(k_dec[:, :, None]),
               bc1(c_dec[:, None, None]))
    _, _, k_dec1, c_dec1 = _decay_tables(1)
    decay_s = (bc1(k_dec1[:, :, None]), bc1(c_dec1[:, None, None]))

    xp = x_prompt.reshape(batch * seq, D_MODEL)
    xs = x_sample.reshape(nb, D_MODEL)
    conv_p, ret_p = [], []
    bufs = None
    state_conv_t = jnp.transpose(state_conv, (0, 2, 1, 3))
    for layer in range(DEPTH):
        m_mix, cst, rst, ffn_b = _prompt_mix(
            xp, weights_b[0], weights_b[1], tab_p, (conv_w_rows,) + conv_params[1:], layer, decay_p,
            batch, seq, weights_f32[2:] if layer == 0 else ())
        w_in_b, w_out_b, w_gu_b, w_dn_b = weights_b + ffn_b
        conv_p.append(cst)
        ret_p.append(rst)
        xp, weights_b = _prompt_ffn(m_mix, xp, ln1_g, ln1_b, w_gu_b, w_dn_b, ln2_g, ln2_b,
                                    layer, weights_f32 if layer + 1 < DEPTH else ())

        u, q, k, v, sg = _mix_in(xs, w_in_b, tab_s)
        x1, cso, rso = _mix_core_sample(u, q, k, v, sg, xs, state_conv_t, state_ret,
                                        conv_params + (w_out_b, ln1_g, ln1_b), layer, decay_s, bufs)
        bufs = (cso, rso)
        xs = _ffn(x1, w_gu_b, w_dn_b, ln2_g, ln2_b, layer)

    return (xp.reshape(batch, seq, D_MODEL), xs.reshape(nb, 1, D_MODEL),
            jnp.stack(conv_p, axis=0), jnp.stack(ret_p, axis=0),
            jnp.transpose(bufs[0], (0, 2, 1, 3)), bufs[1])
```

```python
import functools

import jax
import jax.numpy as jnp
import numpy as np
from jax import lax
from jax.experimental import pallas as pl
from jax.experimental.pallas import tpu as pltpu

F32 = jnp.float32
BF16 = jnp.bfloat16

D_MODEL = 1024
DEPTH = 4
CONV_CH = 512
CONV_K = 31
HEADS = 4
HEAD_DIM = 128
RET_W = HEADS * HEAD_DIM
D_FF = 2816
IN_COLS = 2 * CONV_CH + 4 * RET_W
PAST_LEN = 16384
ROPE_BASE = 10000.0
DN_ALPHA = (2 * DEPTH) ** 0.25
LN_EPS = 1e-5
CHUNK = 256

SUBLANES = 8
VMEM_LIMIT_BYTES = 56 * 1024 * 1024

HIST = 32
CONV_OFF = HIST - (CONV_K - 1)
CONV_ROWS = 64
FF_CHUNK_W = 256
FF_CHUNKS = tuple((s, min(FF_CHUNK_W, D_FF - s)) for s in range(0, D_FF, FF_CHUNK_W))
CAST_STEPS = 2

PROMPT_TILE = 512
FFN_TILE = 512
FFN_PIECES = (0, 176, 352, 512)
SAMPLE_TILE = 16


def _dot(a, b):
    return jnp.dot(a, b, preferred_element_type=F32)


def _dot_nt(a, b):
    return lax.dot_general(a, b, (((1,), (1,)), ((), ())), preferred_element_type=F32)


def _layer_norm(y, g, b):
    mu = jnp.mean(y, axis=-1, keepdims=True)
    d = y - mu
    var = jnp.mean(d * d, axis=-1, keepdims=True)
    return d * lax.rsqrt(var + LN_EPS) * g + b


def _group_norm(o, g):
    mu = jnp.mean(o, axis=-1, keepdims=True)
    d = o - mu
    var = jnp.mean(d * d, axis=-1, keepdims=True)
    return d * lax.rsqrt(var + LN_EPS) * g


def _silu(x):
    return x * jax.nn.sigmoid(x)


def _row(ref, layer):
    return ref[layer:layer + 1, :]


def _rotary(seg, cos, sin_signed):
    return seg * cos + pltpu.roll(seg, HEAD_DIM // 2, 1) * sin_signed


def _shift_copies(win_ref, sh_ref, lo, hi):
    for r in range(1, SUBLANES):
        sh_ref[r - 1, lo:hi, :] = win_ref[r + lo:r + hi, :]


def _conv_block(win_ref, sh_ref, cw_ref, cbias, clg, clb, r0):
    acc = jnp.broadcast_to(cbias, (CONV_ROWS, CONV_CH))
    for j in range(CONV_K):
        a, r = divmod(CONV_OFF + j, SUBLANES)
        lo = r0 + a * SUBLANES
        src = win_ref[lo:lo + CONV_ROWS, :] if r == 0 else sh_ref[r - 1, lo:lo + CONV_ROWS, :]
        acc = acc + src * jnp.concatenate([cw_ref[j]] * (CONV_ROWS // SUBLANES), axis=0)
    return _silu(_layer_norm(acc, clg, clb)).astype(BF16)


def _swiglu_ffn(x1, wgu_ref, wdn_ref, a_scr):
    xb = x1.astype(BF16)
    for s, w in FF_CHUNKS:
        gate = _dot(xb, wgu_ref[:, s:s + w])
        up = _dot(xb, wgu_ref[:, D_FF + s:D_FF + s + w])
        a_scr[:, s:s + w] = (_silu(gate) * up).astype(BF16)
    return _dot(a_scr[...], wdn_ref[...])


def _cast_specs(weights, layer, n_steps, step):
    n_slabs = n_steps // CAST_STEPS
    ins, outs, shapes = [], [], []
    for w in weights:
        _, rows, cols = w.shape
        slab = rows // n_slabs
        ins.append(pl.BlockSpec((None, slab, cols), lambda *g: (layer, step(*g) // CAST_STEPS, 0)))
        outs.append(pl.BlockSpec((None, slab, cols), lambda *g: (0, step(*g) // CAST_STEPS, 0)))
        shapes.append(jax.ShapeDtypeStruct((1, rows, cols), BF16))
    return ins, outs, shapes


def _cast_slabs(src_refs, dst_refs):
    for src_ref, dst_ref in zip(src_refs, dst_refs):
        dst_ref[...] = src_ref[...].astype(BF16)


def _prompt_mix_kernel(x_ref, w_ref, cq_ref, sq_ref, ck_ref, sk_ref,
                       cw_ref, cb_ref, clg_ref, clb_ref, gn_ref, wout_ref,
                       dmask_ref, qdec_ref, kdec_ref, cdec_ref, *rest, tm, layer):
    n_cast = (len(rest) - 12) // 2
    m_ref, convst_ref, retst_ref = rest[n_cast:n_cast + 3]
    ubuf, shbuf, cr_ref, s_scr, q_scr, k_scr, kd_scr, v_scr, sg_scr = rest[2 * n_cast + 3:]
    l = pl.program_id(1)

    @pl.when(l == 0)
    def _():
        ubuf[0:HIST, :] = jnp.zeros((HIST, CONV_CH), F32)
        s_scr[...] = jnp.zeros_like(s_scr)

    _cast_slabs(rest[:n_cast], rest[n_cast + 3:2 * n_cast + 3])
    xb = x_ref[...].astype(BF16)
    n_chunks = tm // CHUNK

    half = tm // 2
    halo = HIST - SUBLANES

    def glu(p):
        rows = slice(p * half, (p + 1) * half)
        ca = _dot(xb[rows, :], w_ref[:, 0:CONV_CH])
        cb = _dot(xb[rows, :], w_ref[:, CONV_CH:2 * CONV_CH])
        ubuf[HIST + p * half:HIST + (p + 1) * half, :] = ca * jax.nn.sigmoid(cb)
        _shift_copies(ubuf, shbuf, p * half + (halo if p else 0), (p + 1) * half + halo)

    clg, clb, cbias = _row(clg_ref, layer), _row(clb_ref, layer), _row(cb_ref, layer)

    def conv_block(rb):
        r0 = rb * CONV_ROWS
        cr_ref[r0:r0 + CONV_ROWS, 0:CONV_CH] = _conv_block(ubuf, shbuf, cw_ref, cbias, clg, clb, r0)

    o = 2 * CONV_CH
    cq, sq, ck, sk = cq_ref[...], sq_ref[...], ck_ref[...], sk_ref[...]

    def proj_q():
        zq = _dot(xb, w_ref[:, o:o + RET_W])
        for h in range(HEADS):
            cols = slice(h * HEAD_DIM, (h + 1) * HEAD_DIM)
            q_scr[:, cols] = _rotary(zq[:, cols], cq, sq).astype(BF16)

    def proj_k():
        zk = _dot(xb, w_ref[:, o + RET_W:o + 2 * RET_W])
        for h in range(HEADS):
            cols = slice(h * HEAD_DIM, (h + 1) * HEAD_DIM)
            kr = _rotary(zk[:, cols], ck, sk)
            k_scr[:, cols] = kr.astype(BF16)
            for c in range(n_chunks):
                rows = slice(c * CHUNK, (c + 1) * CHUNK)
                kd_scr[rows, cols] = (kr[rows, :] * kdec_ref[h]).astype(BF16)

    def proj_v():
        v_scr[...] = _dot(xb, w_ref[:, o + 2 * RET_W:o + 3 * RET_W]).astype(BF16)

    def proj_gate():
        sg_scr[...] = _silu(_dot(xb, w_ref[:, o + 3 * RET_W:o + 4 * RET_W]))

    def retention(c, h):
        rows = slice(c * CHUNK, (c + 1) * CHUNK)
        cols = slice(h * HEAD_DIM, (h + 1) * HEAD_DIM)
        qc = q_scr[rows, cols]
        vc = v_scr[rows, cols]
        s = s_scr[h]
        sc = _dot_nt(qc, k_scr[rows, cols]) * dmask_ref[h]
        out = _dot(sc.astype(BF16), vc) + _dot(qc, s.astype(BF16)) * qdec_ref[h]
        s_scr[h] = s * cdec_ref[h] + _dot(kd_scr[rows, cols].T, vc)
        on = _group_norm(out, gn_ref[layer:layer + 1, cols])
        cr_ref[rows, CONV_CH + h * HEAD_DIM:CONV_CH + (h + 1) * HEAD_DIM] = (
            sg_scr[rows, cols] * on).astype(BF16)

    def out_proj(c):
        rows = slice(c * CHUNK, (c + 1) * CHUNK)
        m_ref[rows, :] = (_dot(cr_ref[rows, 0:CONV_CH], wout_ref[0:CONV_CH, :])
                          + _dot(cr_ref[rows, CONV_CH:D_MODEL], wout_ref[CONV_CH:D_MODEL, :]))

    glu(0)
    glu(1)
    stages = [proj_q, proj_k, proj_v, proj_gate]
    proj_stage = []
    for c in range(n_chunks):
        stages += [functools.partial(retention, c, h) for h in range(HEADS)]
        proj_stage.append(len(stages))
        stages.append(functools.partial(out_proj, c))
    n_blocks = tm // CONV_ROWS
    blocks_per_chunk = CHUNK // CONV_ROWS
    assert all((c + 1) * blocks_per_chunk <= proj_stage[c] for c in range(n_chunks))
    for p, stage in enumerate(stages):
        stage()
        if p < n_blocks:
            conv_block(p)

    ubuf[0:HIST, :] = ubuf[tm:tm + HIST, :]

    @pl.when(l == pl.num_programs(1) - 1)
    def _():
        convst_ref[...] = ubuf[CONV_OFF:HIST, :]
        retst_ref[...] = s_scr[...]


def _prompt_mix(x, w_in_b, w_out_b, tables, params, layer, decay, batch, seq, cast_weights):
    conv_w_rows, conv_b, conv_ln_g, conv_ln_b, ret_gn_g = params
    tm = PROMPT_TILE
    nl = seq // tm
    row = lambda b, l: (b * nl + l, 0)
    vec = lambda w: pl.BlockSpec((DEPTH, w), lambda b, l: (0, 0))
    tab_spec = pl.BlockSpec((tm, HEAD_DIM), lambda b, l: (l, 0))
    dec_spec = lambda rows, cols: pl.BlockSpec((HEADS, rows, cols), lambda b, l: (0, 0, 0))
    cast_in, cast_out, cast_shapes = _cast_specs(cast_weights, layer, batch * nl,
                                                 lambda b, l: b * nl + l)
    outs = pl.pallas_call(
        functools.partial(_prompt_mix_kernel, tm=tm, layer=layer),
        grid=(batch, nl),
        in_specs=[pl.BlockSpec((tm, D_MODEL), row),
                  pl.BlockSpec((None, D_MODEL, IN_COLS), lambda b, l: (0, 0, 0),
                               pipeline_mode=pl.Buffered(1)),
                  tab_spec, tab_spec, tab_spec, tab_spec,
                  pl.BlockSpec((None, CONV_K, SUBLANES, CONV_CH), lambda b, l: (layer, 0, 0, 0)),
                  vec(CONV_CH), vec(CONV_CH), vec(CONV_CH), vec(RET_W),
                  pl.BlockSpec((None, D_MODEL, D_MODEL), lambda b, l: (0, 0, 0),
                               pipeline_mode=pl.Buffered(1)),
                  dec_spec(CHUNK, CHUNK), dec_spec(CHUNK, HEAD_DIM), dec_spec(CHUNK, HEAD_DIM),
                  dec_spec(HEAD_DIM, HEAD_DIM)] + cast_in,
        out_specs=[pl.BlockSpec((tm, D_MODEL), row),
                   pl.BlockSpec((None, CONV_K - 1, CONV_CH), lambda b, l: (b, 0, 0)),
                   pl.BlockSpec((None, HEADS, HEAD_DIM, HEAD_DIM), lambda b, l: (b, 0, 0, 0))]
        + cast_out,
        out_shape=[jax.ShapeDtypeStruct((batch * seq, D_MODEL), F32),
                   jax.ShapeDtypeStruct((batch, CONV_K - 1, CONV_CH), F32),
                   jax.ShapeDtypeStruct((batch, HEADS, HEAD_DIM, HEAD_DIM), F32)] + cast_shapes,
        scratch_shapes=[pltpu.VMEM((HIST + tm, CONV_CH), F32),
                        pltpu.VMEM((SUBLANES - 1, HIST + tm - SUBLANES, CONV_CH), F32),
                        pltpu.VMEM((tm, D_MODEL), BF16),
                        pltpu.VMEM((HEADS, HEAD_DIM, HEAD_DIM), F32),
                        pltpu.VMEM((tm, RET_W), BF16),
                        pltpu.VMEM((tm, RET_W), BF16),
                        pltpu.VMEM((tm, RET_W), BF16),
                        pltpu.VMEM((tm, RET_W), BF16),
                        pltpu.VMEM((tm, RET_W), F32)],
        compiler_params=pltpu.CompilerParams(dimension_semantics=("arbitrary", "arbitrary"),
                                             vmem_limit_bytes=VMEM_LIMIT_BYTES),
        name="prompt_mix",
    )(x, w_in_b, *tables, conv_w_rows, conv_b, conv_ln_g, conv_ln_b, ret_gn_g, w_out_b, *decay,
      *cast_weights)
    return outs[0], outs[1], outs[2], tuple(outs[3:])


def _prompt_ffn_kernel(m_ref, x_ref, l1g_ref, l1b_ref, wgu_ref, wdn_ref, l2g_ref, l2b_ref,
                       *rest, tm, layer):
    n_cast = (len(rest) - 2) // 2
    o_ref, a_scr = rest[n_cast], rest[-1]
    _cast_slabs(rest[:n_cast], rest[n_cast + 1:-1])

    rows = tuple(slice(a, b) for a, b in zip(FFN_PIECES[:-1], FFN_PIECES[1:]))
    l1g, l1b, l2g, l2b = (_row(r, layer) for r in (l1g_ref, l1b_ref, l2g_ref, l2b_ref))

    def norm1(p):
        return _layer_norm(DN_ALPHA * x_ref[rows[p], :] + m_ref[rows[p], :], l1g, l1b)

    def ff_chunk(p, xb, k):
        s, w = FF_CHUNKS[k]
        gate = _dot(xb, wgu_ref[:, s:s + w])
        up = _dot(xb, wgu_ref[:, D_FF + s:D_FF + s + w])
        a_scr[rows[p], s:s + w] = (_silu(gate) * up).astype(BF16)

    def down(p):
        return _dot(a_scr[rows[p], :], wdn_ref[...])

    def norm2(p, x1, f):
        o_ref[rows[p], :] = _layer_norm(DN_ALPHA * x1 + f, l2g, l2b)

    n_ff = len(FF_CHUNKS)
    x1 = norm1(0)
    prev = None
    for p in range(len(rows)):
        xb = x1.astype(BF16)
        ff_chunk(p, xb, 0)
        if prev is not None:
            norm2(*prev)
        x1_next = norm1(p + 1) if p + 1 < len(rows) else None
        for k in range(1, n_ff):
            ff_chunk(p, xb, k)
        prev = (p, x1, down(p))
        x1 = x1_next
    norm2(*prev)


def _prompt_ffn(m_mix, x, ln1_g, ln1_b, w_gu_b, w_dn_b, ln2_g, ln2_b, layer, next_weights):
    m = x.shape[0]
    tm = FFN_TILE
    n_steps = m // tm
    row = lambda i: (i, 0)
    fixed = lambda i: (0, 0, 0)
    once = pl.Buffered(1)
    vec = pl.BlockSpec((DEPTH, D_MODEL), lambda i: (0, 0))
    cast_in, cast_out, cast_shapes = _cast_specs(next_weights, layer + 1, n_steps, lambda i: i)
    outs = pl.pallas_call(
        functools.partial(_prompt_ffn_kernel, tm=tm, layer=layer),
        grid=(n_steps,),
        in_specs=[pl.BlockSpec((tm, D_MODEL), row),
                  pl.BlockSpec((tm, D_MODEL), row),
                  vec, vec,
                  pl.BlockSpec((None, D_MODEL, 2 * D_FF), fixed, pipeline_mode=once),
                  pl.BlockSpec((None, D_FF, D_MODEL), fixed, pipeline_mode=once),
                  vec, vec] + cast_in,
        out_specs=[pl.BlockSpec((tm, D_MODEL), row)] + cast_out,
        out_shape=[jax.ShapeDtypeStruct((m, D_MODEL), F32)] + cast_shapes,
        scratch_shapes=[pltpu.VMEM((tm, D_FF), BF16)],
        compiler_params=pltpu.CompilerParams(dimension_semantics=("arbitrary",),
                                             vmem_limit_bytes=VMEM_LIMIT_BYTES),
        name="prompt_ffn",
    )(m_mix, x, ln1_g, ln1_b, w_gu_b, w_dn_b, ln2_g, ln2_b, *next_weights)
    return outs[0], tuple(outs[1:])


def _mix_in_kernel(x_ref, w_ref, cq_ref, sq_ref, ck_ref, sk_ref,
                   u_ref, q_ref, k_ref, v_ref, sg_ref):
    xb = x_ref[...].astype(BF16)
    ca = _dot(xb, w_ref[:, 0:CONV_CH])
    cb = _dot(xb, w_ref[:, CONV_CH:2 * CONV_CH])
    u_ref[...] = ca * jax.nn.sigmoid(cb)
    o = 2 * CONV_CH
    zq = _dot(xb, w_ref[:, o:o + RET_W])
    zk = _dot(xb, w_ref[:, o + RET_W:o + 2 * RET_W])
    cq, sq, ck, sk = cq_ref[...], sq_ref[...], ck_ref[...], sk_ref[...]
    for h in range(HEADS):
        sl = slice(h * HEAD_DIM, (h + 1) * HEAD_DIM)
        q_ref[:, sl] = _rotary(zq[:, sl], cq, sq).astype(BF16)
        k_ref[:, sl] = _rotary(zk[:, sl], ck, sk)
    v_ref[...] = _dot(xb, w_ref[:, o + 2 * RET_W:o + 3 * RET_W]).astype(BF16)
    sg_ref[...] = _silu(_dot(xb, w_ref[:, o + 3 * RET_W:o + 4 * RET_W]))


def _mix_in(x, w_in_b, tables):
    m = x.shape[0]
    full = lambda w: pl.BlockSpec((m, w), lambda i: (0, 0))
    return pl.pallas_call(
        _mix_in_kernel,
        grid=(1,),
        in_specs=[full(D_MODEL),
                  pl.BlockSpec((None, D_MODEL, IN_COLS), lambda i: (0, 0, 0)),
                  full(HEAD_DIM), full(HEAD_DIM), full(HEAD_DIM), full(HEAD_DIM)],
        out_specs=[full(RET_W)] * 5,
        out_shape=[jax.ShapeDtypeStruct((m, CONV_CH), F32),
                   jax.ShapeDtypeStruct((m, RET_W), BF16),
                   jax.ShapeDtypeStruct((m, RET_W), F32),
                   jax.ShapeDtypeStruct((m, RET_W), BF16),
                   jax.ShapeDtypeStruct((m, RET_W), F32)],
        compiler_params=pltpu.CompilerParams(dimension_semantics=("arbitrary",),
                                             vmem_limit_bytes=VMEM_LIMIT_BYTES),
        name="mix_in",
    )(x, w_in_b, *tables)


def _mix_core_sample_kernel(*refs, tb, aliased, layer):
    if aliased:
        refs = refs[:18] + refs[20:]
    (u_ref, q_ref, k_ref, v_ref, sg_ref, x_ref, cst_ref, rst_ref,
     cw_ref, cb_ref, clg_ref, clb_ref, gn_ref, wout_ref, l1g_ref, l1b_ref,
     kdec_ref, cdec_ref,
     x1_ref, cso_ref, rso_ref,
     o_scr) = refs

    u = u_ref[...]
    conv = u * cw_ref[CONV_K - 1:CONV_K, :] + _row(cb_ref, layer)
    for j in range(CONV_K - 1):
        conv = conv + cst_ref[j] * cw_ref[j:j + 1, :]
        if j > 0:
            cso_ref[j - 1] = cst_ref[j]
    cso_ref[CONV_K - 2] = u
    c = _silu(_layer_norm(conv, _row(clg_ref, layer), _row(clb_ref, layer))).astype(BF16)

    wide = tb * HEAD_DIM
    in_block = (lax.broadcasted_iota(jnp.int32, (tb, wide), 1) // HEAD_DIM
                == lax.broadcasted_iota(jnp.int32, (tb, wide), 0))
    ones_bd = jnp.where(in_block, 1.0, 0.0).astype(BF16)
    contract_rows = (((0,), (0,)), ((), ()))
    for h in range(HEADS):
        cols = slice(h * HEAD_DIM, (h + 1) * HEAD_DIM)
        kd = (k_ref[:, cols] * kdec_ref[h, 0:tb, :]).astype(BF16)
        v_bd = jnp.where(in_block, jnp.concatenate([v_ref[:, cols].astype(F32)] * tb, axis=1),
                         0.0).astype(BF16)
        kv = lax.dot_general(kd, v_bd, contract_rows, preferred_element_type=F32)
        q_cols = lax.dot_general(q_ref[:, cols], ones_bd, contract_rows,
                                 preferred_element_type=F32)
        for b in range(tb):
            blk = slice(b * HEAD_DIM, (b + 1) * HEAD_DIM)
            s_new = rst_ref[b, h] * cdec_ref[h] + kv[:, blk]
            rso_ref[b, h] = s_new
            o_scr[b:b + 1, cols] = jnp.sum(q_cols[:, blk] * s_new, axis=0, keepdims=True)

    rs = []
    for h in range(HEADS):
        cols = slice(h * HEAD_DIM, (h + 1) * HEAD_DIM)
        on = _group_norm(o_scr[:, cols], gn_ref[layer:layer + 1, cols])
        rs.append((sg_ref[:, cols] * on).astype(BF16))
    cr = jnp.concatenate([c] + rs, axis=-1)
    m = _dot(cr, wout_ref[...])
    x1_ref[...] = _layer_norm(DN_ALPHA * x_ref[...] + m, _row(l1g_ref, layer), _row(l1b_ref, layer))


def _mix_core_sample(u, q, k, v, sg, x, state_conv, state_ret, params, layer, decay, bufs):
    conv_w, conv_b, conv_ln_g, conv_ln_b, ret_gn_g, w_out_b, ln1_g, ln1_b = params
    nb = x.shape[0]
    tb = SAMPLE_TILE
    row = lambda i: (i, 0)
    lay2 = lambda i: (layer, 0, 0)
    act = lambda w: pl.BlockSpec((tb, w), row)
    vec = lambda w: pl.BlockSpec((DEPTH, w), lambda i: (0, 0))
    dec_spec = pl.BlockSpec((HEADS, HEAD_DIM, HEAD_DIM), lambda i: (0, 0, 0))
    cst_spec = pl.BlockSpec((None, CONV_K - 1, tb, CONV_CH), lambda i: (layer, 0, i, 0))
    rst_spec = pl.BlockSpec((None, tb, HEADS, HEAD_DIM, HEAD_DIM), lambda i: (layer, i, 0, 0, 0))
    in_specs = [act(CONV_CH), act(RET_W), act(RET_W), act(RET_W), act(RET_W), act(D_MODEL),
                cst_spec, rst_spec,
                pl.BlockSpec((None, CONV_K, CONV_CH), lay2),
                vec(CONV_CH), vec(CONV_CH), vec(CONV_CH), vec(RET_W),
                pl.BlockSpec((None, D_MODEL, D_MODEL), lambda i: (0, 0, 0)),
                vec(D_MODEL), vec(D_MODEL),
                dec_spec, dec_spec]
    args = [u, q, k, v, sg, x, state_conv, state_ret,
            conv_w, conv_b, conv_ln_g, conv_ln_b, ret_gn_g, w_out_b, ln1_g, ln1_b, *decay]
    aliases = {}
    if bufs is not None:
        in_specs += [pl.BlockSpec(memory_space=pl.ANY), pl.BlockSpec(memory_space=pl.ANY)]
        args += list(bufs)
        aliases = {18: 1, 19: 2}
    return pl.pallas_call(
        functools.partial(_mix_core_sample_kernel, tb=tb, aliased=bufs is not None, layer=layer),
        grid=(nb // tb,),
        in_specs=in_specs,
        out_specs=[act(D_MODEL), cst_spec, rst_spec],
        out_shape=[jax.ShapeDtypeStruct((nb, D_MODEL), F32),
                   jax.ShapeDtypeStruct(state_conv.shape, F32),
                   jax.ShapeDtypeStruct(state_ret.shape, F32)],
        scratch_shapes=[pltpu.VMEM((tb, RET_W), F32)],
        input_output_aliases=aliases,
        compiler_params=pltpu.CompilerParams(dimension_semantics=("arbitrary",),
                                             vmem_limit_bytes=VMEM_LIMIT_BYTES),
        name="mix_core_sample",
    )(*args)


def _ffn_kernel(x_ref, wgu_ref, wdn_ref, g_ref, b_ref, o_ref, a_scr, *, layer):
    x = x_ref[...]
    f = _swiglu_ffn(x, wgu_ref, wdn_ref, a_scr)
    o_ref[...] = _layer_norm(DN_ALPHA * x + f, _row(g_ref, layer), _row(b_ref, layer))


def _ffn(x, w_gu_b, w_dn_b, ln2_g, ln2_b, layer):
    m = x.shape[0]
    once = pl.Buffered(1)
    vec = pl.BlockSpec((DEPTH, D_MODEL), lambda i: (0, 0))
    return pl.pallas_call(
        functools.partial(_ffn_kernel, layer=layer),
        grid=(1,),
        in_specs=[pl.BlockSpec((m, D_MODEL), lambda i: (0, 0)),
                  pl.BlockSpec((None, D_MODEL, 2 * D_FF), lambda i: (0, 0, 0), pipeline_mode=once),
                  pl.BlockSpec((None, D_FF, D_MODEL), lambda i: (0, 0, 0), pipeline_mode=once),
                  vec, vec],
        out_specs=pl.BlockSpec((m, D_MODEL), lambda i: (0, 0)),
        out_shape=jax.ShapeDtypeStruct((m, D_MODEL), F32),
        scratch_shapes=[pltpu.VMEM((m, D_FF), BF16)],
        compiler_params=pltpu.CompilerParams(dimension_semantics=("arbitrary",),
                                             vmem_limit_bytes=VMEM_LIMIT_BYTES),
        name="ffn",
    )(x, w_gu_b, w_dn_b, ln2_g, ln2_b)


def _rotary_tables(pos, rows):
    half = HEAD_DIM // 2
    inv_freq = ROPE_BASE ** (-np.arange(0, half, dtype=np.float64) / half)
    ang = np.asarray(pos, np.float64)[:, None] * inv_freq[None, :]
    cos, sin = np.cos(ang), np.sin(ang)
    cos2 = np.concatenate([cos, cos], axis=-1)
    sin2 = np.concatenate([-sin, sin], axis=-1)
    scale = HEAD_DIM ** -0.5
    return tuple(jnp.asarray(np.broadcast_to(t, (rows, HEAD_DIM)).astype(np.float32))
                 for t in (cos2, sin2, cos2 * scale, sin2 * scale))


def _decay_tables(chunk):
    log_g = np.log(1.0 - 2.0 ** (-5.0 - np.arange(HEADS, dtype=np.float64)))
    idx = np.arange(chunk, dtype=np.float64)
    diff = idx[:, None] - idx[None, :]
    dmask = np.where(diff[None] >= 0,
                     np.exp(np.maximum(diff, 0.0)[None] * log_g[:, None, None]), 0.0)
    q_decay = np.exp((idx + 1.0)[None, :] * log_g[:, None])
    k_decay = np.exp((chunk - 1.0 - idx)[None, :] * log_g[:, None])
    chunk_decay = np.exp(chunk * log_g)
    return tuple(t.astype(np.float32) for t in (dmask, q_decay, k_decay, chunk_decay))


def kernel(x_prompt, x_sample, state_conv, state_ret, w_in, conv_w, conv_b, conv_ln_g, conv_ln_b,
           ret_gn_g, w_out, ln1_g, ln1_b, w_gate_up, w_down, ln2_g, ln2_b):
    batch, seq, _ = x_prompt.shape
    nb = x_sample.shape[0]
    weights_f32 = (w_in, w_out, w_gate_up, w_down)
    weights_b = (w_in[0:1].astype(BF16), w_out[0:1].astype(BF16))
    conv_params = (conv_w, conv_b, conv_ln_g, conv_ln_b, ret_gn_g)
    conv_w_rows = jnp.broadcast_to(conv_w[:, :, None, :], (DEPTH, CONV_K, SUBLANES, CONV_CH))

    tab_p = _rotary_tables(np.arange(seq), seq)
    tab_s = _rotary_tables(PAST_LEN + np.arange(1), nb)

    dmask, q_dec, k_dec, c_dec = _decay_tables(CHUNK)
    bc = lambda a: jnp.asarray(np.broadcast_to(a, (HEADS, CHUNK, HEAD_DIM)))
    bc1 = lambda a: jnp.asarray(np.broadcast_to(a, (HEADS, HEAD_DIM, HEAD_DIM)))
    decay_p = (jnp.asarray(dmask), bc(q_dec[:, :, None]), bc(k_dec[:, :, None]),
               bc1(c_dec[:, None, None]))
    _, _, k_dec1, c_dec1 = _decay_tables(1)
    decay_s = (bc1(k_dec1[:, :, None]), bc1(c_dec1[:, None, None]))

    xp = x_prompt.reshape(batch * seq, D_MODEL)
    xs = x_sample.reshape(nb, D_MODEL)
    conv_p, ret_p = [], []
    bufs = None
    state_conv_t = jnp.transpose(state_conv, (0, 2, 1, 3))
    for layer in range(DEPTH):
        m_mix, cst, rst, ffn_b = _prompt_mix(
            xp, weights_b[0], weights_b[1], tab_p, (conv_w_rows,) + conv_params[1:], layer, decay_p,
            batch, seq, weights_f32[2:] if layer == 0 else ())
        w_in_b, w_out_b, w_gu_b, w_dn_b = weights_b + ffn_b
        conv_p.append(cst)
        ret_p.append(rst)
        xp, weights_b = _prompt_ffn(m_mix, xp, ln1_g, ln1_b, w_gu_b, w_dn_b, ln2_g, ln2_b,
                                    layer, weights_f32 if layer + 1 < DEPTH else ())

        u, q, k, v, sg = _mix_in(xs, w_in_b, tab_s)
        x1, cso, rso = _mix_core_sample(u, q, k, v, sg, xs, state_conv_t, state_ret,
                                        conv_params + (w_out_b, ln1_g, ln1_b), layer, decay_s, bufs)
        bufs = (cso, rso)
        xs = _ffn(x1, w_gu_b, w_dn_b, ln2_g, ln2_b, layer)

    return (xp.reshape(batch, seq, D_MODEL), xs.reshape(nb, 1, D_MODEL),
            jnp.stack(conv_p, axis=0), jnp.stack(ret_p, axis=0),
            jnp.transpose(bufs[0], (0, 2, 1, 3)), bufs[1])
```
